```python
import jax, jax.numpy as jnp
from jax import lax
import numpy as np

D_MODEL = 2048
BATCH = 4
SEQ = 2048
DEPTH = 1
DEC_BATCH = 32
DEC_SEQ = 4
PAST_LEN = 16384
PAGE_SIZE = 128

D_MIX = D_MODEL
D_ATTN = D_MIX // 2
HEAD_DIM = 128
N_HEADS = D_ATTN // HEAD_DIM
D_POOL = D_MIX - D_ATTN
POOL_WINDOWS = (2, 4, 8, 16)
N_POOL_GROUPS = len(POOL_WINDOWS)
POOL_GROUP = D_POOL // N_POOL_GROUPS
POOL_HIST = max(POOL_WINDOWS) - 1
D_FF = ((8 * D_MODEL + 3 * 256 - 1) // (3 * 256)) * 256
D_IN = 3 * D_ATTN + D_POOL
Q_BLOCK = 128
SB_BIAS_INIT = -8.0
EPS = 1e-6

kernel_name = "hymba_stickbreak_pool_decode_step"


def rmsnorm(x, g):
    x32 = x.astype(jnp.float32)
    y = x32 * lax.rsqrt(jnp.mean(x32 * x32, axis=-1, keepdims=True) + EPS)
    return (y * g.astype(jnp.float32)).astype(x.dtype)


def mixer_inputs(x, norm1_g, w_in, q_norm_g, k_norm_g):
    b, t, _ = x.shape
    h = rmsnorm(x, norm1_g)
    u = jnp.einsum('btd,de->bte', h, w_in)
    q = u[..., :D_ATTN].reshape(b, t, N_HEADS, HEAD_DIM)
    k = u[..., D_ATTN:2 * D_ATTN].reshape(b, t, N_HEADS, HEAD_DIM)
    v = u[..., 2 * D_ATTN:3 * D_ATTN].reshape(b, t, N_HEADS, HEAD_DIM)
    p = u[..., 3 * D_ATTN:]
    q = rmsnorm(q, q_norm_g)
    k = rmsnorm(k, k_norm_g)
    return q, k, v, p


def stick_breaking(q, k, v, sb_bias, q_pos, k_pos):
    z = jnp.einsum('bthd,blhd->bhtl', q.astype(jnp.float32), k.astype(jnp.float32)) * (HEAD_DIM ** -0.5)
    z = z + sb_bias.astype(jnp.float32)[None, :, None, None]
    mask = (k_pos[None, :] < q_pos[:, None])[None, None]
    c = jnp.where(mask, -jax.nn.softplus(z), 0.0)
    log_rem = lax.cumsum(c, axis=3, reverse=True) - c
    a = jnp.where(mask, jnp.exp(jax.nn.log_sigmoid(z) + log_rem), 0.0)
    return jnp.einsum('bhtl,blhd->bthd', a, v.astype(jnp.float32))


def pool_mix(p_ext, pos_new, pool_w, pool_scale):
    b, l, _ = p_ext.shape
    t = l - POOL_HIST
    p32 = p_ext.astype(jnp.float32)
    csum = jnp.concatenate([jnp.zeros((b, 1, D_POOL), jnp.float32), jnp.cumsum(p32, axis=1)], axis=1)
    new = p32[:, POOL_HIST:]
    groups = []
    for g, w in enumerate(POOL_WINDOWS):
        lo, hi = g * POOL_GROUP, (g + 1) * POOL_GROUP
        s = csum[:, POOL_HIST + 1:POOL_HIST + 1 + t, lo:hi] - csum[:, POOL_HIST + 1 - w:POOL_HIST + 1 - w + t, lo:hi]
        cnt = jnp.minimum(pos_new + 1, w).astype(jnp.float32)[None, :, None]
        groups.append(s / cnt - new[:, :, lo:hi])
    d = jnp.stack(groups, axis=2)
    y = jnp.einsum('btgc,gcd->btgd', d, pool_w.astype(jnp.float32)).reshape(b, t, D_POOL)
    return y * pool_scale.astype(jnp.float32)


def finish_layer(x, o_attn, o_pool, w_out, norm2_g, w_gate, w_up, w_down):
    b, t, _ = x.shape
    mix = jnp.concatenate([o_attn.reshape(b, t, D_ATTN), o_pool], axis=-1).astype(x.dtype)
    x = x + jnp.einsum('bte,ed->btd', mix, w_out)
    h = rmsnorm(x, norm2_g)
    ff = jax.nn.silu(jnp.einsum('btd,df->btf', h, w_gate)) * jnp.einsum('btd,df->btf', h, w_up)
    return x + jnp.einsum('btf,fd->btd', ff, w_down)


def setup_inputs(seed: int = 0) -> dict:
    key = jax.random.key(seed)
    ks = jax.random.split(key, 20)
    n_pages = PAST_LEN // PAGE_SIZE
    n_used = DEC_BATCH * n_pages
    n_pool_pages = n_used + (n_used + 3) // 4
    f32 = jnp.float32

    def nrm(k, shape, scale=1.0):
        return jax.random.normal(k, shape, f32) * scale

    perm = jax.random.permutation(ks[0], n_pool_pages)[:n_used]
    page_table = perm.reshape(DEC_BATCH, n_pages).astype(jnp.int32)
    return {
        "x_prompt": nrm(ks[1], (BATCH, SEQ, D_MODEL)),
        "x_sample": nrm(ks[2], (DEC_BATCH, DEC_SEQ, D_MODEL)),
        "cache_k": nrm(ks[3], (DEPTH, n_pool_pages, PAGE_SIZE, N_HEADS, HEAD_DIM)),
        "cache_v": nrm(ks[4], (DEPTH, n_pool_pages, PAGE_SIZE, N_HEADS, HEAD_DIM)),
        "state_pool": nrm(ks[5], (DEPTH, DEC_BATCH, POOL_HIST, D_POOL)),
        "page_table": page_table,
        "norm1_g": 1.0 + nrm(ks[6], (DEPTH, D_MODEL), 0.05),
        "w_in": nrm(ks[7], (DEPTH, D_MODEL, D_IN), D_MODEL ** -0.5),
        "q_norm_g": 1.0 + nrm(ks[8], (DEPTH, HEAD_DIM), 0.05),
        "k_norm_g": 1.0 + nrm(ks[9], (DEPTH, HEAD_DIM), 0.05),
        "sb_bias": SB_BIAS_INIT + nrm(ks[17], (DEPTH, N_HEADS), 0.1),
        "pool_w": nrm(ks[10], (DEPTH, N_POOL_GROUPS, POOL_GROUP, POOL_GROUP), POOL_GROUP ** -0.5),
        "pool_scale": 1.0 + nrm(ks[11], (DEPTH, D_POOL), 0.1),
        "w_out": nrm(ks[12], (DEPTH, D_MIX, D_MODEL), D_MIX ** -0.5),
        "norm2_g": 1.0 + nrm(ks[13], (DEPTH, D_MODEL), 0.05),
        "w_gate": nrm(ks[14], (DEPTH, D_MODEL, D_FF), D_MODEL ** -0.5),
        "w_up": nrm(ks[15], (DEPTH, D_MODEL, D_FF), D_MODEL ** -0.5),
        "w_down": nrm(ks[16], (DEPTH, D_FF, D_MODEL), D_FF ** -0.5),
    }


def reference(x_prompt, x_sample, cache_k, cache_v, state_pool, page_table,
              norm1_g, w_in, q_norm_g, k_norm_g, sb_bias, pool_w, pool_scale, w_out,
              norm2_g, w_gate, w_up, w_down):
    b_p, seq, _ = x_prompt.shape
    b_s, dec_seq, _ = x_sample.shape
    n_pages = page_table.shape[1]
    page = cache_k.shape[2]
    past_len = n_pages * page

    pos_p = jnp.arange(seq, dtype=jnp.int32)
    pos_s_new = past_len + jnp.arange(dec_seq, dtype=jnp.int32)
    pos_s_keys = jnp.arange(past_len + dec_seq, dtype=jnp.int32)

    xp, xs = x_prompt, x_sample
    kp_l, vp_l, pp_l, ks_l, vs_l, ps_l = [], [], [], [], [], []
    for l in range(DEPTH):
        q, k, v, p = mixer_inputs(xp, norm1_g[l], w_in[l], q_norm_g[l], k_norm_g[l])
        blocks = []
        for b0 in range(0, seq, Q_BLOCK):
            end = b0 + Q_BLOCK
            blocks.append(stick_breaking(q[:, b0:end], k[:, :end], v[:, :end], sb_bias[l], pos_p[b0:end], pos_p[:end]))
        o_attn = jnp.concatenate(blocks, axis=1)
        p_ext = jnp.concatenate([jnp.zeros((b_p, POOL_HIST, D_POOL), p.dtype), p], axis=1)
        o_pool = pool_mix(p_ext, pos_p, pool_w[l], pool_scale[l])
        kp_l.append(k)
        vp_l.append(v)
        pp_l.append(p_ext[:, -POOL_HIST:])
        xp = finish_layer(xp, o_attn, o_pool, w_out[l], norm2_g[l], w_gate[l], w_up[l], w_down[l])

        q, k, v, p = mixer_inputs(xs, norm1_g[l], w_in[l], q_norm_g[l], k_norm_g[l])
        past_k = cache_k[l][page_table].reshape(b_s, past_len, N_HEADS, HEAD_DIM).astype(k.dtype)
        past_v = cache_v[l][page_table].reshape(b_s, past_len, N_HEADS, HEAD_DIM).astype(v.dtype)
        k_all = jnp.concatenate([past_k, k], axis=1)
        v_all = jnp.concatenate([past_v, v], axis=1)
        o_attn = stick_breaking(q, k_all, v_all, sb_bias[l], pos_s_new, pos_s_keys)
        p_ext = jnp.concatenate([state_pool[l].astype(p.dtype), p], axis=1)
        o_pool = pool_mix(p_ext, pos_s_new, pool_w[l], pool_scale[l])
        ks_l.append(k)
        vs_l.append(v)
        ps_l.append(p_ext[:, -POOL_HIST:])
        xs = finish_layer(xs, o_attn, o_pool, w_out[l], norm2_g[l], w_gate[l], w_up[l], w_down[l])

    new_k_prompt = jnp.stack(kp_l, axis=0)
    new_v_prompt = jnp.stack(vp_l, axis=0)
    new_pool_prompt = jnp.stack(pp_l, axis=0)
    new_k_sample = jnp.stack(ks_l, axis=0)
    new_v_sample = jnp.stack(vs_l, axis=0)
    new_pool_sample = jnp.stack(ps_l, axis=0)
    return (xp, xs, new_k_prompt, new_v_prompt, new_pool_prompt, new_k_sample, new_v_sample, new_pool_sample)
```

```python
import functools

import jax
import jax.numpy as jnp
from jax import lax
from jax.experimental import pallas as pl
from jax.experimental.pallas import tpu as pltpu

F32 = jnp.float32
BF16 = jnp.bfloat16

HEAD_DIM = 128
POOL_WINDOWS = (2, 4, 8, 16)
POOL_HIST = max(POOL_WINDOWS) - 1
HALO = 16
EPS = 1e-6
VMEM_LIMIT = 56 * 1024 * 1024
PAGES_PER_STEP = 8


def _params(*sem):
    return pltpu.CompilerParams(dimension_semantics=sem, vmem_limit_bytes=VMEM_LIMIT)


def _softplus(z):
    return jnp.maximum(z, 0.0) + jnp.log(1.0 + jnp.exp(-jnp.abs(z)))


def _split_dot(c, tri):
    hi = c.astype(BF16)
    lo = (c - hi.astype(F32)).astype(BF16)
    n = c.shape[0]
    out = jnp.dot(jnp.concatenate([hi, lo], axis=0), tri, preferred_element_type=F32)
    return out[:n] + out[n:]


def _later_key_matrix(n):
    j = lax.broadcasted_iota(jnp.int32, (n, n), 0)
    s = lax.broadcasted_iota(jnp.int32, (n, n), 1)
    return (j > s).astype(BF16)


def _in_proj_kernel(x_ref, g1_ref, w_ref, qg_ref, kg_ref, q_ref, k_ref, v_ref, p_ref, h_ref):
    j = pl.program_id(1)

    @pl.when(j == 0)
    def _():
        x = x_ref[...]
        ms = jnp.mean(x * x, axis=-1, keepdims=True)
        h_ref[...] = (x * lax.rsqrt(ms + EPS) * g1_ref[...]).astype(BF16)

    n_chunk = w_ref.shape[1] // 256

    def project(c):
        return jnp.dot(h_ref[...], w_ref[:, c * 256:(c + 1) * 256], preferred_element_type=F32)

    def head_norm(u, g):
        outs = []
        for i in range(u.shape[1] // HEAD_DIM):
            blk = u[:, i * HEAD_DIM:(i + 1) * HEAD_DIM]
            ms = jnp.mean(blk * blk, axis=-1, keepdims=True)
            outs.append(blk * lax.rsqrt(ms + EPS) * g)
        return jnp.concatenate(outs, axis=1)

    @pl.when(j == 0)
    def _():
        for c in range(n_chunk):
            q_ref[:, c * 256:(c + 1) * 256] = head_norm(project(c), qg_ref[...]).astype(BF16)

    @pl.when(j == 1)
    def _():
        for c in range(n_chunk):
            k_ref[:, c * 256:(c + 1) * 256] = head_norm(project(c), kg_ref[...])

    @pl.when(j == 2)
    def _():
        for c in range(n_chunk):
            v_ref[:, c * 256:(c + 1) * 256] = project(c)

    @pl.when(j == 3)
    def _():
        for c in range(n_chunk):
            p_ref[:, c * 256:(c + 1) * 256] = project(c)


def _in_proj(x, g1, w_in, qg, kg, tm):
    rows, d = x.shape
    sec = w_in.shape[1] // 4
    out_spec = pl.BlockSpec((tm, sec), lambda i, j: (i, 0))
    return pl.pallas_call(
        _in_proj_kernel,
        grid=(rows // tm, 4),
        in_specs=[
            pl.BlockSpec((tm, d), lambda i, j: (i, 0)),
            pl.BlockSpec((1, d), lambda i, j: (0, 0)),
            pl.BlockSpec((d, sec), lambda i, j: (0, j)),
            pl.BlockSpec((1, HEAD_DIM), lambda i, j: (0, 0)),
            pl.BlockSpec((1, HEAD_DIM), lambda i, j: (0, 0)),
        ],
        out_specs=[out_spec, out_spec, out_spec, out_spec],
        out_shape=[
            jax.ShapeDtypeStruct((rows, sec), BF16),
            jax.ShapeDtypeStruct((rows, sec), F32),
            jax.ShapeDtypeStruct((rows, sec), F32),
            jax.ShapeDtypeStruct((rows, sec), F32),
        ],
        scratch_shapes=[pltpu.VMEM((tm, d), BF16)],
        compiler_params=_params("arbitrary", "arbitrary"),
        name="in_proj",
    )(x, g1, w_in, qg, kg)


def _prompt_attn_kernel(bias_ref, q_ref, k_ref, v_ref, o_ref, kb_ref, vb_ref, *, blk):
    h = pl.program_id(1)
    qi = pl.program_id(2)

    @pl.when(qi == 0)
    def _():
        kb_ref[...] = k_ref[...].astype(BF16)
        vb_ref[...] = v_ref[...].astype(BF16)

    bias = bias_ref[h]
    scale = HEAD_DIM ** -0.5
    q = q_ref[...]
    tri = _later_key_matrix(blk)

    def block(start, run, acc, diagonal):
        kb = kb_ref[pl.ds(start, blk), :]
        vb = vb_ref[pl.ds(start, blk), :]
        z = lax.dot_general(q, kb, (((1,), (1,)), ((), ())), preferred_element_type=F32) * scale + bias
        sp = _softplus(z)
        if diagonal:
            row = lax.broadcasted_iota(jnp.int32, (blk, blk), 0)
            col = lax.broadcasted_iota(jnp.int32, (blk, blk), 1)
            mask = col < row
            c = jnp.where(mask, -sp, 0.0)
        else:
            c = -sp
        log_rem = _split_dot(c, tri) + run
        a = jnp.exp(z - sp + log_rem)
        if diagonal:
            a = jnp.where(mask, a, 0.0)
        acc = acc + jnp.dot(a.astype(BF16), vb, preferred_element_type=F32)
        run = run + jnp.sum(c, axis=-1, keepdims=True)
        return run, acc

    run0 = jnp.zeros((blk, 1), F32)
    acc0 = jnp.zeros((blk, HEAD_DIM), F32)
    run, acc = block(pl.multiple_of(qi * blk, blk), run0, acc0, True)

    def body(i, carry):
        start = pl.multiple_of((qi - 1 - i) * blk, blk)
        return block(start, carry[0], carry[1], False)

    run, acc = lax.fori_loop(0, qi, body, (run, acc))
    o_ref[...] = acc.astype(BF16)


def _prompt_attn(q, k, v, sb_bias, blk):
    b, t, da = q.shape
    n_heads = da // HEAD_DIM
    qo_spec = pl.BlockSpec((None, blk, HEAD_DIM), lambda bi, h, qi: (bi, qi, h))
    kv_spec = pl.BlockSpec((None, t, HEAD_DIM), lambda bi, h, qi: (bi, 0, h))
    return pl.pallas_call(
        functools.partial(_prompt_attn_kernel, blk=blk),
        grid=(b, n_heads, t // blk),
        in_specs=[pl.BlockSpec(memory_space=pltpu.SMEM), qo_spec, kv_spec, kv_spec],
        out_specs=qo_spec,
        out_shape=jax.ShapeDtypeStruct((b, t, da), BF16),
        scratch_shapes=[pltpu.VMEM((t, HEAD_DIM), BF16), pltpu.VMEM((t, HEAD_DIM), BF16)],
        compiler_params=_params("arbitrary", "arbitrary", "arbitrary"),
        name="prompt_attn",
    )(sb_bias, q, k, v)


def _pool_delta_kernel(halo_ref, p_ref, d_ref, ext_ref, *, blocks_per_seq, base_pos):
    i = pl.program_id(0)
    tm = p_ref.shape[0]
    group = p_ref.shape[1] // len(POOL_WINDOWS)
    seq_blk = i % blocks_per_seq

    ext_ref[pl.ds(HALO, tm), :] = p_ref[...]

    @pl.when(seq_blk == 0)
    def _():
        ext_ref[pl.ds(0, HALO), :] = jnp.zeros((HALO, p_ref.shape[1]), F32) if base_pos == 0 else halo_ref[...]

    @pl.when(seq_blk != 0)
    def _():
        ext_ref[pl.ds(0, HALO), :] = halo_ref[...]

    pos = base_pos + seq_blk * tm + lax.broadcasted_iota(jnp.int32, (tm, 1), 0)
    for g, w in enumerate(POOL_WINDOWS):
        cols = pl.ds(g * group, group)
        s = ext_ref[pl.ds(HALO, tm), cols]
        for back in range(1, w):
            s = s + ext_ref[pl.ds(HALO - back, tm), cols]
        cnt = jnp.minimum(pos + 1, w).astype(F32)
        d_ref[:, cols] = (s / cnt - ext_ref[pl.ds(HALO, tm), cols]).astype(BF16)


def _pool_delta(halo_src, p, tm, blocks_per_seq, base_pos, halo_index):
    n_blocks, _, c = p.shape
    return pl.pallas_call(
        functools.partial(_pool_delta_kernel, blocks_per_seq=blocks_per_seq, base_pos=base_pos),
        grid=(n_blocks,),
        in_specs=[
            pl.BlockSpec((None, HALO, c), lambda i: (halo_index(i), 0, 0)),
            pl.BlockSpec((None, tm, c), lambda i: (i, 0, 0)),
        ],
        out_specs=pl.BlockSpec((None, tm, c), lambda i: (i, 0, 0)),
        out_shape=jax.ShapeDtypeStruct((n_blocks, tm, c), BF16),
        scratch_shapes=[pltpu.VMEM((HALO + tm, c), F32)],
        compiler_params=_params("arbitrary"),
        name="pool_delta",
    )(halo_src, p)


def _mix_out_kernel(x_ref, oa_ref, d_ref, pw_ref, ps_ref, wo_ref, g2_ref, x1_ref, h2_ref):
    da = oa_ref.shape[1]
    n_groups, group, _ = pw_ref.shape
    pooled = []
    for g in range(n_groups):
        y = jnp.dot(d_ref[:, g * group:(g + 1) * group], pw_ref[g], preferred_element_type=F32)
        pooled.append((y * ps_ref[:, g * group:(g + 1) * group]).astype(BF16))
    o_pool = jnp.concatenate(pooled, axis=1)
    y = jnp.dot(oa_ref[...], wo_ref[:da, :], preferred_element_type=F32)
    y = y + jnp.dot(o_pool, wo_ref[da:, :], preferred_element_type=F32)
    x1 = x_ref[...] + y
    x1_ref[...] = x1
    ms = jnp.mean(x1 * x1, axis=-1, keepdims=True)
    h2_ref[...] = (x1 * lax.rsqrt(ms + EPS) * g2_ref[...]).astype(BF16)


def _mix_out(x, oa, d, pool_w, pool_scale, w_out, g2, tm):
    rows, dm = x.shape
    da = oa.shape[1]
    dp = d.shape[1]
    row = lambda i: (i, 0)
    fixed2 = lambda i: (0, 0)
    return pl.pallas_call(
        _mix_out_kernel,
        grid=(rows // tm,),
        in_specs=[
            pl.BlockSpec((tm, dm), row),
            pl.BlockSpec((tm, da), row),
            pl.BlockSpec((tm, dp), row),
            pl.BlockSpec(pool_w.shape, lambda i: (0, 0, 0)),
            pl.BlockSpec((1, dp), fixed2),
            pl.BlockSpec(w_out.shape, fixed2),
            pl.BlockSpec((1, dm), fixed2),
        ],
        out_specs=[pl.BlockSpec((tm, dm), row), pl.BlockSpec((tm, dm), row)],
        out_shape=[jax.ShapeDtypeStruct((rows, dm), F32), jax.ShapeDtypeStruct((rows, dm), BF16)],
        compiler_params=_params("arbitrary"),
        name="mix_out",
    )(x, oa, d, pool_w, pool_scale, w_out, g2)


def _ffn_kernel(x1_ref, h2_ref, wg_ref, wu_ref, wd_ref, o_ref):
    f = pl.program_id(1)

    @pl.when(f == 0)
    def _():
        o_ref[...] = x1_ref[...]

    h2 = h2_ref[...]
    g = jnp.dot(h2, wg_ref[...], preferred_element_type=F32)
    u = jnp.dot(h2, wu_ref[...], preferred_element_type=F32)
    ff = (g / (1.0 + jnp.exp(-g))) * u
    o_ref[...] += jnp.dot(ff.astype(BF16), wd_ref[...], preferred_element_type=F32)


def _ffn(x1, h2, wg, wu, wd, tm, tf):
    rows, dm = x1.shape
    dff = wg.shape[1]
    return pl.pallas_call(
        _ffn_kernel,
        grid=(rows // tm, dff // tf),
        in_specs=[
            pl.BlockSpec((tm, dm), lambda i, f: (i, 0)),
            pl.BlockSpec((tm, dm), lambda i, f: (i, 0)),
            pl.BlockSpec((dm, tf), lambda i, f: (0, f)),
            pl.BlockSpec((dm, tf), lambda i, f: (0, f)),
            pl.BlockSpec((tf, dm), lambda i, f: (f, 0)),
        ],
        out_specs=pl.BlockSpec((tm, dm), lambda i, f: (i, 0)),
        out_shape=jax.ShapeDtypeStruct((rows, dm), F32),
        compiler_params=_params("arbitrary", "arbitrary"),
        name="ffn",
    )(x1, h2, wg, wu, wd)


def _sample_attn_kernel(pt_ref, q_ref, kn_ref, vn_ref, bias_ref, *refs, n_heads, page, pages_per_step):
    del pt_ref
    k_refs = refs[:pages_per_step]
    v_refs = refs[pages_per_step:2 * pages_per_step]
    o_ref, acc_ref, run_ref, qbd_ref, kpad_ref, vpad_ref = refs[2 * pages_per_step:]
    j = pl.program_id(1)
    n_q = q_ref.shape[0]
    rows = n_q * n_heads
    da = n_heads * HEAD_DIM
    scale = HEAD_DIM ** -0.5
    tri = _later_key_matrix(page)
    head_of_row = lax.broadcasted_iota(jnp.int32, (rows, da), 0) % n_heads
    head_of_col = lax.broadcasted_iota(jnp.int32, (rows, da), 1) // HEAD_DIM
    own_head = head_of_row == head_of_col

    def update(kcat, vcat, run, acc, mask):
        z = lax.dot_general(qbd_ref[...], kcat, (((1,), (1,)), ((), ())), preferred_element_type=F32)
        z = z * scale + bias_ref[...]
        sp = _softplus(z)
        c = -sp if mask is None else jnp.where(mask, -sp, 0.0)
        a = jnp.exp(z - sp + _split_dot(c, tri) + run)
        if mask is not None:
            a = jnp.where(mask, a, 0.0)
        acc = acc + jnp.dot(a.astype(BF16), vcat, preferred_element_type=F32)
        run = run + jnp.sum(c, axis=-1, keepdims=True)
        return run, acc

    @pl.when(j == 0)
    def _():
        q = q_ref[...]
        q_rows = jnp.concatenate(
            [jnp.broadcast_to(q[t:t + 1, :], (n_heads, da)) for t in range(n_q)], axis=0)
        qbd_ref[...] = jnp.where(own_head, q_rows, 0.0).astype(BF16)
        n_new = kn_ref.shape[0]
        kpad_ref[...] = jnp.zeros(kpad_ref.shape, F32)
        vpad_ref[...] = jnp.zeros(vpad_ref.shape, F32)
        kpad_ref[pl.ds(0, n_new), :] = kn_ref[...]
        vpad_ref[pl.ds(0, n_new), :] = vn_ref[...]
        t_of_row = lax.broadcasted_iota(jnp.int32, (rows, page), 0) // n_heads
        key = lax.broadcasted_iota(jnp.int32, (rows, page), 1)
        run, acc = update(kpad_ref[...].astype(BF16), vpad_ref[...].astype(BF16),
                          jnp.zeros((rows, 1), F32), jnp.zeros((rows, da), F32), key < t_of_row)
        run_ref[...] = run
        acc_ref[...] = acc

    run = run_ref[...]
    acc = acc_ref[...]
    for k_ref, v_ref in zip(k_refs, v_refs):
        kcat = jnp.concatenate(
            [k_ref[pl.ds(h, page, stride=n_heads), :] for h in range(n_heads)], axis=1).astype(BF16)
        vcat = jnp.concatenate(
            [v_ref[pl.ds(h, page, stride=n_heads), :] for h in range(n_heads)], axis=1).astype(BF16)
        run, acc = update(kcat, vcat, run, acc, None)
    run_ref[...] = run
    acc_ref[...] = acc

    @pl.when(j == pl.num_programs(1) - 1)
    def _():
        kept = jnp.where(own_head, acc, 0.0)
        o_ref[...] = jnp.sum(kept.reshape(n_q, n_heads, da), axis=1)


def _sample_attn(q, k_new, v_new, bias_rows, cache_k, cache_v, page_table):
    b, n_q, da = q.shape
    n_heads = da // HEAD_DIM
    rows = n_q * n_heads
    n_pages = page_table.shape[1]
    page = cache_k.shape[1] // n_heads
    pps = PAGES_PER_STEP
    steps = n_pages // pps

    def page_spec(i):
        return pl.BlockSpec(
            (None, page * n_heads, HEAD_DIM),
            lambda bi, j, pt, i=i: (pt[bi, n_pages - 1 - (j * pps + i)], 0, 0))

    per_batch = lambda bi, j, pt: (bi, 0, 0)
    grid_spec = pltpu.PrefetchScalarGridSpec(
        num_scalar_prefetch=1,
        grid=(b, steps),
        in_specs=[
            pl.BlockSpec((None, n_q, da), per_batch),
            pl.BlockSpec((None, k_new.shape[1], da), per_batch),
            pl.BlockSpec((None, v_new.shape[1], da), per_batch),
            pl.BlockSpec((rows, page), lambda bi, j, pt: (0, 0)),
        ] + [page_spec(i) for i in range(pps)] * 2,
        out_specs=pl.BlockSpec((None, n_q, da), per_batch),
        scratch_shapes=[
            pltpu.VMEM((rows, da), F32),
            pltpu.VMEM((rows, 1), F32),
            pltpu.VMEM((rows, da), BF16),
            pltpu.VMEM((page, da), F32),
            pltpu.VMEM((page, da), F32),
        ],
    )
    return pl.pallas_call(
        functools.partial(_sample_attn_kernel, n_heads=n_heads, page=page, pages_per_step=pps),
        grid_spec=grid_spec,
        out_shape=jax.ShapeDtypeStruct((b, n_q, da), F32),
        compiler_params=_params("arbitrary", "arbitrary"),
        name="sample_attn",
    )(page_table, q, k_new, v_new, bias_rows, *([cache_k] * pps), *([cache_v] * pps))


def _row_tile(rows, target):
    return target if rows % target == 0 else rows


def _finish(x2d, o_attn, d, lw, tm):
    x1, h2 = _mix_out(x2d, o_attn, d, lw["pool_w"], lw["pool_scale"], lw["w_out"], lw["norm2_g"], tm)
    return _ffn(x1, h2, lw["w_gate"], lw["w_up"], lw["w_down"], tm, 512)


def kernel(x_prompt, x_sample, cache_k, cache_v, state_pool, page_table, norm1_g, w_in, q_norm_g, k_norm_g,
           sb_bias, pool_w, pool_scale, w_out, norm2_g, w_gate, w_up, w_down):
    b_p, seq, dm = x_prompt.shape
    b_s, dec_seq, _ = x_sample.shape
    depth = w_in.shape[0]
    n_pool_pages, page, n_heads, _ = cache_k.shape[1:]
    da = n_heads * HEAD_DIM
    dp = state_pool.shape[-1]
    past_len = page_table.shape[1] * page

    xp = x_prompt.reshape(b_p * seq, dm)
    xs = x_sample.reshape(b_s * dec_seq, dm)
    tm_p = _row_tile(b_p * seq, 512)
    tm_s = b_s * dec_seq
    outs = [[] for _ in range(6)]
    for l in range(depth):
        lw = {
            "pool_w": pool_w[l].astype(BF16),
            "pool_scale": pool_scale[l].reshape(1, dp),
            "w_out": w_out[l].astype(BF16),
            "norm2_g": norm2_g[l].reshape(1, dm),
            "w_gate": w_gate[l].astype(BF16),
            "w_up": w_up[l].astype(BF16),
            "w_down": w_down[l].astype(BF16),
        }
        g1 = norm1_g[l].reshape(1, dm)
        w_in_l = w_in[l].astype(BF16)
        qg = q_norm_g[l].reshape(1, HEAD_DIM)
        kg = k_norm_g[l].reshape(1, HEAD_DIM)

        q, k, v, p = _in_proj(xp, g1, w_in_l, qg, kg, tm_p)
        o_attn = _prompt_attn(q.reshape(b_p, seq, da), k.reshape(b_p, seq, da), v.reshape(b_p, seq, da),
                              sb_bias[l], 256)
        blocks_per_seq = seq // tm_p
        halo_per_block = tm_p // HALO
        d = _pool_delta(p.reshape(-1, HALO, dp), p.reshape(-1, tm_p, dp), tm_p, blocks_per_seq, 0,
                        lambda i: jnp.maximum(i * halo_per_block - 1, 0))
        outs[0].append(k.reshape(b_p, seq, n_heads, HEAD_DIM))
        outs[1].append(v.reshape(b_p, seq, n_heads, HEAD_DIM))
        outs[2].append(p.reshape(b_p, seq, dp)[:, seq - POOL_HIST:])
        xp = _finish(xp, o_attn.reshape(b_p * seq, da), d.reshape(b_p * seq, dp), lw, tm_p)

        q, k, v, p = _in_proj(xs, g1, w_in_l, qg, kg, tm_s)
        pad = 8 - dec_seq
        k_new = jnp.pad(k.reshape(b_s, dec_seq, da), ((0, 0), (0, pad), (0, 0)))
        v_new = jnp.pad(v.reshape(b_s, dec_seq, da), ((0, 0), (0, pad), (0, 0)))
        bias_rows = jnp.broadcast_to(jnp.tile(sb_bias[l].astype(F32), dec_seq)[:, None], (dec_seq * n_heads, page))
        o_attn = _sample_attn(q.astype(F32).reshape(b_s, dec_seq, da), k_new, v_new, bias_rows,
                              cache_k[l].reshape(n_pool_pages, page * n_heads, HEAD_DIM),
                              cache_v[l].reshape(n_pool_pages, page * n_heads, HEAD_DIM), page_table)
        p3 = p.reshape(b_s, dec_seq, dp)
        hist = jnp.pad(state_pool[l], ((0, 0), (HALO - POOL_HIST, 0), (0, 0)))
        d = _pool_delta(hist, jnp.pad(p3, ((0, 0), (0, pad), (0, 0))), 8, 1, past_len, lambda i: i)
        outs[3].append(k.reshape(b_s, dec_seq, n_heads, HEAD_DIM))
        outs[4].append(v.reshape(b_s, dec_seq, n_heads, HEAD_DIM))
        outs[5].append(jnp.concatenate([state_pool[l], p3], axis=1)[:, dec_seq:])
        xs = _finish(xs, o_attn.astype(BF16).reshape(b_s * dec_seq, da), d[:, :dec_seq].reshape(b_s * dec_seq, dp),
                     lw, tm_s)

    stack = lambda xs_: jnp.stack(xs_, axis=0)
    return (xp.reshape(b_p, seq, dm), xs.reshape(b_s, dec_seq, dm), stack(outs[0]), stack(outs[1]),
            stack(outs[2]), stack(outs[3]), stack(outs[4]), stack(outs[5]))
```

```python
import functools

import jax
import jax.numpy as jnp
from jax import lax
from jax.experimental import pallas as pl
from jax.experimental.pallas import tpu as pltpu

F32 = jnp.float32
BF16 = jnp.bfloat16

HEAD_DIM = 128
POOL_WINDOWS = (2, 4, 8, 16)
POOL_HIST = max(POOL_WINDOWS) - 1
HALO = 16
EPS = 1e-6
VMEM_LIMIT = 56 * 1024 * 1024
PAGES_PER_STEP = 8


def _params(*sem):
    return pltpu.CompilerParams(dimension_semantics=sem, vmem_limit_bytes=VMEM_LIMIT)


LOG2E = 1.4426950408889634
SCORE_SCALE = HEAD_DIM ** -0.5 * LOG2E
MASKED_SCORE = -1e30


def _softplus2(y):
    return jnp.maximum(y, 0.0) + jnp.log(1.0 + jnp.exp2(-jnp.abs(y))) * LOG2E


def _split_bf16(x):
    hi = x.astype(BF16)
    lo = (x - hi.astype(F32)).astype(BF16)
    return jnp.concatenate([hi, lo], axis=1)


def _minus_later_keys(n, page=None):
    j = lax.broadcasted_iota(jnp.int32, (n, n), 0)
    s = lax.broadcasted_iota(jnp.int32, (n, n), 1)
    if page is None:
        later = j > s
    else:
        later = (j // page < s // page) | ((j // page == s // page) & (j > s))
    m = jnp.where(later, -1.0, 0.0).astype(BF16)
    return jnp.concatenate([m, m], axis=0)


def _in_proj_kernel(x_ref, g1_ref, w_ref, qg_ref, kg_ref, q_ref, k_ref, v_ref, p_ref, h_ref):
    j = pl.program_id(1)

    @pl.when(j == 0)
    def _():
        x = x_ref[...]
        ms = jnp.mean(x * x, axis=-1, keepdims=True)
        h_ref[...] = (x * lax.rsqrt(ms + EPS) * g1_ref[...]).astype(BF16)

    n_chunk = w_ref.shape[1] // 256

    def project(c):
        return jnp.dot(h_ref[...], w_ref[:, c * 256:(c + 1) * 256], preferred_element_type=F32)

    def head_norm(u, g):
        outs = []
        for i in range(u.shape[1] // HEAD_DIM):
            blk = u[:, i * HEAD_DIM:(i + 1) * HEAD_DIM]
            ms = jnp.mean(blk * blk, axis=-1, keepdims=True)
            outs.append(blk * lax.rsqrt(ms + EPS) * g)
        return jnp.concatenate(outs, axis=1)

    @pl.when(j == 0)
    def _():
        for c in range(n_chunk):
            q_ref[:, c * 256:(c + 1) * 256] = head_norm(project(c), qg_ref[...]).astype(BF16)

    @pl.when(j == 1)
    def _():
        for c in range(n_chunk):
            k_ref[:, c * 256:(c + 1) * 256] = head_norm(project(c), kg_ref[...])

    @pl.when(j == 2)
    def _():
        for c in range(n_chunk):
            v_ref[:, c * 256:(c + 1) * 256] = project(c)

    @pl.when(j == 3)
    def _():
        for c in range(n_chunk):
            p_ref[:, c * 256:(c + 1) * 256] = project(c)


def _in_proj(x, g1, w_in, qg, kg, tm):
    rows, d = x.shape
    sec = w_in.shape[1] // 4
    out_spec = pl.BlockSpec((tm, sec), lambda i, j: (i, 0))
    return pl.pallas_call(
        _in_proj_kernel,
        grid=(rows // tm, 4),
        in_specs=[
            pl.BlockSpec((tm, d), lambda i, j: (i, 0)),
            pl.BlockSpec((1, d), lambda i, j: (0, 0)),
            pl.BlockSpec((d, sec), lambda i, j: (0, j)),
            pl.BlockSpec((1, HEAD_DIM), lambda i, j: (0, 0)),
            pl.BlockSpec((1, HEAD_DIM), lambda i, j: (0, 0)),
        ],
        out_specs=[out_spec, out_spec, out_spec, out_spec],
        out_shape=[
            jax.ShapeDtypeStruct((rows, sec), BF16),
            jax.ShapeDtypeStruct((rows, sec), F32),
            jax.ShapeDtypeStruct((rows, sec), F32),
            jax.ShapeDtypeStruct((rows, sec), F32),
        ],
        scratch_shapes=[pltpu.VMEM((tm, d), BF16)],
        compiler_params=_params("arbitrary", "arbitrary"),
        name="in_proj",
    )(x, g1, w_in, qg, kg)


def _prompt_attn_kernel(bias_ref, q_ref, k_ref, v_ref, o_ref, kb_ref, vb_ref, *, blk):
    h = pl.program_id(1)
    n_q_blocks = q_ref.shape[0] // blk
    kb_ref[...] = k_ref[...].astype(BF16)
    vb_ref[...] = v_ref[...].astype(BF16)
    later = _minus_later_keys(blk)
    bias = bias_ref[h] * LOG2E
    row = lax.broadcasted_iota(jnp.int32, (blk, blk), 0)
    col = lax.broadcasted_iota(jnp.int32, (blk, blk), 1)
    causal = col < row

    def stage_a(q, start, diagonal):
        kb = kb_ref[pl.ds(start, blk), :]
        y = lax.dot_general(q, kb, (((1,), (1,)), ((), ())), preferred_element_type=F32)
        y = y * SCORE_SCALE + bias
        if diagonal:
            y = jnp.where(causal, y, MASKED_SCORE)
        sp = _softplus2(y)
        return y - sp, _split_bf16(sp), jnp.sum(sp, axis=-1, keepdims=True)

    def stage_b(start, t, s, run, acc):
        rem = jnp.dot(s, later, preferred_element_type=F32)
        a = jnp.exp2(t + rem)
        pv = jnp.dot(a.astype(BF16), vb_ref[pl.ds(start, blk), :], preferred_element_type=F32)
        return acc + pv * jnp.exp2(run)

    def query_block(qi, _):
        q0 = pl.multiple_of(qi * blk, blk)
        q = q_ref[pl.ds(q0, blk), :]
        t, s, spent = stage_a(q, q0, True)

        def body(n, carry):
            run, acc, spent_prev, t_prev, s_prev = carry
            start = pl.multiple_of((qi - n) * blk, blk)
            t, s, spent = stage_a(q, start, False)
            acc = stage_b(start + blk, t_prev, s_prev, run, acc)
            return run - spent_prev, acc, spent, t, s

        init = (jnp.zeros((blk, 1), F32), jnp.zeros((blk, HEAD_DIM), F32), spent, t, s)
        run, acc, _, t, s = lax.fori_loop(1, qi + 1, body, init)
        o_ref[pl.ds(q0, blk), :] = stage_b(0, t, s, run, acc).astype(BF16)
        return 0

    lax.fori_loop(0, n_q_blocks, query_block, 0)


def _prompt_attn(q, k, v, sb_bias, blk):
    b, t, da = q.shape
    spec = pl.BlockSpec((None, t, HEAD_DIM), lambda bi, h: (bi, 0, h))
    return pl.pallas_call(
        functools.partial(_prompt_attn_kernel, blk=blk),
        grid=(b, da // HEAD_DIM),
        in_specs=[pl.BlockSpec(memory_space=pltpu.SMEM), spec, spec, spec],
        out_specs=spec,
        out_shape=jax.ShapeDtypeStruct((b, t, da), BF16),
        scratch_shapes=[
            pltpu.VMEM((t, HEAD_DIM), BF16),
            pltpu.VMEM((t, HEAD_DIM), BF16),
        ],
        compiler_params=_params("arbitrary", "arbitrary"),
        name="prompt_attn",
    )(sb_bias, q, k, v)


def _pool_delta_kernel(halo_ref, p_ref, d_ref, ext_ref, *, blocks_per_seq, base_pos):
    i = pl.program_id(0)
    tm = p_ref.shape[0]
    group = p_ref.shape[1] // len(POOL_WINDOWS)
    seq_blk = i % blocks_per_seq

    ext_ref[pl.ds(HALO, tm), :] = p_ref[...]

    @pl.when(seq_blk == 0)
    def _():
        ext_ref[pl.ds(0, HALO), :] = jnp.zeros((HALO, p_ref.shape[1]), F32) if base_pos == 0 else halo_ref[...]

    @pl.when(seq_blk != 0)
    def _():
        ext_ref[pl.ds(0, HALO), :] = halo_ref[...]

    pos = base_pos + seq_blk * tm + lax.broadcasted_iota(jnp.int32, (tm, 1), 0)
    for g, w in enumerate(POOL_WINDOWS):
        cols = pl.ds(g * group, group)
        s = ext_ref[pl.ds(HALO, tm), cols]
        for back in range(1, w):
            s = s + ext_ref[pl.ds(HALO - back, tm), cols]
        cnt = jnp.minimum(pos + 1, w).astype(F32)
        d_ref[:, cols] = (s / cnt - ext_ref[pl.ds(HALO, tm), cols]).astype(BF16)


def _pool_delta(halo_src, p, tm, blocks_per_seq, base_pos, halo_index):
    n_blocks, _, c = p.shape
    return pl.pallas_call(
        functools.partial(_pool_delta_kernel, blocks_per_seq=blocks_per_seq, base_pos=base_pos),
        grid=(n_blocks,),
        in_specs=[
            pl.BlockSpec((None, HALO, c), lambda i: (halo_index(i), 0, 0)),
            pl.BlockSpec((None, tm, c), lambda i: (i, 0, 0)),
        ],
        out_specs=pl.BlockSpec((None, tm, c), lambda i: (i, 0, 0)),
        out_shape=jax.ShapeDtypeStruct((n_blocks, tm, c), BF16),
        scratch_shapes=[pltpu.VMEM((HALO + tm, c), F32)],
        compiler_params=_params("arbitrary"),
        name="pool_delta",
    )(halo_src, p)


def _mix_out_kernel(x_ref, oa_ref, d_ref, pw_ref, ps_ref, wo_ref, g2_ref, x1_ref, h2_ref):
    da = oa_ref.shape[1]
    n_groups, group, _ = pw_ref.shape
    pooled = []
    for g in range(n_groups):
        y = jnp.dot(d_ref[:, g * group:(g + 1) * group], pw_ref[g], preferred_element_type=F32)
        pooled.append((y * ps_ref[:, g * group:(g + 1) * group]).astype(BF16))
    o_pool = jnp.concatenate(pooled, axis=1)
    y = jnp.dot(oa_ref[...], wo_ref[:da, :], preferred_element_type=F32)
    y = y + jnp.dot(o_pool, wo_ref[da:, :], preferred_element_type=F32)
    x1 = x_ref[...] + y
    x1_ref[...] = x1
    ms = jnp.mean(x1 * x1, axis=-1, keepdims=True)
    h2_ref[...] = (x1 * lax.rsqrt(ms + EPS) * g2_ref[...]).astype(BF16)


def _mix_out(x, oa, d, pool_w, pool_scale, w_out, g2, tm):
    rows, dm = x.shape
    da = oa.shape[1]
    dp = d.shape[1]
    row = lambda i: (i, 0)
    fixed2 = lambda i: (0, 0)
    return pl.pallas_call(
        _mix_out_kernel,
        grid=(rows // tm,),
        in_specs=[
            pl.BlockSpec((tm, dm), row),
            pl.BlockSpec((tm, da), row),
            pl.BlockSpec((tm, dp), row),
            pl.BlockSpec(pool_w.shape, lambda i: (0, 0, 0)),
            pl.BlockSpec((1, dp), fixed2),
            pl.BlockSpec(w_out.shape, fixed2),
            pl.BlockSpec((1, dm), fixed2),
        ],
        out_specs=[pl.BlockSpec((tm, dm), row), pl.BlockSpec((tm, dm), row)],
        out_shape=[jax.ShapeDtypeStruct((rows, dm), F32), jax.ShapeDtypeStruct((rows, dm), BF16)],
        compiler_params=_params("arbitrary"),
        name="mix_out",
    )(x, oa, d, pool_w, pool_scale, w_out, g2)


def _ffn_kernel(x1_ref, h2_ref, wg_ref, wu_ref, wd_ref, o_ref):
    f = pl.program_id(1)

    @pl.when(f == 0)
    def _():
        o_ref[...] = x1_ref[...]

    h2 = h2_ref[...]
    g = jnp.dot(h2, wg_ref[...], preferred_element_type=F32)
    u = jnp.dot(h2, wu_ref[...], preferred_element_type=F32)
    ff = (g / (1.0 + jnp.exp(-g))) * u
    o_ref[...] += jnp.dot(ff.astype(BF16), wd_ref[...], preferred_element_type=F32)


def _ffn(x1, h2, wg, wu, wd, tm, tf):
    rows, dm = x1.shape
    dff = wg.shape[1]
    return pl.pallas_call(
        _ffn_kernel,
        grid=(rows // tm, dff // tf),
        in_specs=[
            pl.BlockSpec((tm, dm), lambda i, f: (i, 0)),
            pl.BlockSpec((tm, dm), lambda i, f: (i, 0)),
            pl.BlockSpec((dm, tf), lambda i, f: (0, f)),
            pl.BlockSpec((dm, tf), lambda i, f: (0, f)),
            pl.BlockSpec((tf, dm), lambda i, f: (f, 0)),
        ],
        out_specs=pl.BlockSpec((tm, dm), lambda i, f: (i, 0)),
        out_shape=jax.ShapeDtypeStruct((rows, dm), F32),
        compiler_params=_params("arbitrary", "arbitrary"),
        name="ffn",
    )(x1, h2, wg, wu, wd)


def _sample_attn_kernel(pt_ref, q_ref, kn_ref, vn_ref, bias_ref, *refs, n_heads, page, pages_per_step):
    del pt_ref
    k_refs = refs[:pages_per_step]
    v_refs = refs[pages_per_step:2 * pages_per_step]
    o_ref, acc_ref, run_ref, qbd_ref, kpad_ref, vpad_ref = refs[2 * pages_per_step:]
    j = pl.program_id(1)
    n_q = q_ref.shape[0]
    rows = n_q * n_heads
    da = n_heads * HEAD_DIM
    group = 2 * page
    n_groups = pages_per_step // 2
    head_of_row = lax.broadcasted_iota(jnp.int32, (rows, da), 0) % n_heads
    head_of_col = lax.broadcasted_iota(jnp.int32, (rows, da), 1) // HEAD_DIM
    own_head = head_of_row == head_of_col
    bias = bias_ref[...] * LOG2E

    def scores(keys, n_tiles):
        y = lax.dot_general(qbd_ref[...], keys, (((1,), (1,)), ((), ())), preferred_element_type=F32)
        return y * SCORE_SCALE + jnp.concatenate([bias] * n_tiles, axis=1)

    def heads_side_by_side(ref):
        return jnp.concatenate(
            [ref[pl.ds(h, page, stride=n_heads), :] for h in range(n_heads)], axis=1).astype(BF16)

    @pl.when(j == 0)
    def _():
        q = q_ref[...]
        q_rows = jnp.concatenate(
            [jnp.broadcast_to(q[t:t + 1, :], (n_heads, da)) for t in range(n_q)], axis=0)
        qbd_ref[...] = jnp.where(own_head, q_rows, 0.0).astype(BF16)
        n_new = kn_ref.shape[0]
        kpad_ref[...] = jnp.zeros(kpad_ref.shape, F32)
        vpad_ref[...] = jnp.zeros(vpad_ref.shape, F32)
        kpad_ref[pl.ds(0, n_new), :] = kn_ref[...]
        vpad_ref[pl.ds(0, n_new), :] = vn_ref[...]
        t_of_row = lax.broadcasted_iota(jnp.int32, (rows, page), 0) // n_heads
        key = lax.broadcasted_iota(jnp.int32, (rows, page), 1)
        causal = key < t_of_row
        y = scores(kpad_ref[...].astype(BF16), 1)
        sp = jnp.where(causal, _softplus2(y), 0.0)
        rem = jnp.dot(_split_bf16(sp), _minus_later_keys(page), preferred_element_type=F32)
        a = jnp.where(causal, jnp.exp2(y - sp + rem), 0.0)
        acc_ref[...] = jnp.dot(a.astype(BF16), vpad_ref[...].astype(BF16), preferred_element_type=F32)
        run_ref[...] = -jnp.sum(sp, axis=-1, keepdims=True)

    keys = jnp.concatenate([heads_side_by_side(r) for r in k_refs], axis=0)
    y = scores(keys, pages_per_step)
    sp = _softplus2(y)
    split = jnp.concatenate(
        [_split_bf16(sp[:, g * group:(g + 1) * group]) for g in range(n_groups)], axis=0)
    rem = jnp.dot(split, _minus_later_keys(group, page), preferred_element_type=F32)
    run = run_ref[...]
    weights = []
    for g in range(n_groups):
        cols = slice(g * group, (g + 1) * group)
        a = jnp.exp2(y[:, cols] - sp[:, cols] + rem[g * rows:(g + 1) * rows]) * jnp.exp2(run)
        weights.append(a.astype(BF16))
        run = run - jnp.sum(sp[:, cols], axis=-1, keepdims=True)
    run_ref[...] = run
    values = jnp.concatenate([heads_side_by_side(r) for r in v_refs], axis=0)
    acc = acc_ref[...] + jnp.dot(jnp.concatenate(weights, axis=1), values, preferred_element_type=F32)
    acc_ref[...] = acc

    @pl.when(j == pl.num_programs(1) - 1)
    def _():
        kept = jnp.where(own_head, acc, 0.0)
        o_ref[...] = jnp.sum(kept.reshape(n_q, n_heads, da), axis=1)


def _sample_attn(q, k_new, v_new, bias_rows, cache_k, cache_v, page_table):
    b, n_q, da = q.shape
    n_heads = da // HEAD_DIM
    rows = n_q * n_heads
    n_pages = page_table.shape[1]
    page = cache_k.shape[1] // n_heads
    pps = PAGES_PER_STEP
    steps = n_pages // pps

    def page_spec(i):
        return pl.BlockSpec(
            (None, page * n_heads, HEAD_DIM),
            lambda bi, j, pt, i=i: (pt[bi, n_pages - 1 - (j * pps + i)], 0, 0))

    per_batch = lambda bi, j, pt: (bi, 0, 0)
    grid_spec = pltpu.PrefetchScalarGridSpec(
        num_scalar_prefetch=1,
        grid=(b, steps),
        in_specs=[
            pl.BlockSpec((None, n_q, da), per_batch),
            pl.BlockSpec((None, k_new.shape[1], da), per_batch),
            pl.BlockSpec((None, v_new.shape[1], da), per_batch),
            pl.BlockSpec((rows, page), lambda bi, j, pt: (0, 0)),
        ] + [page_spec(i) for i in range(pps)] * 2,
        out_specs=pl.BlockSpec((None, n_q, da), per_batch),
        scratch_shapes=[
            pltpu.VMEM((rows, da), F32),
            pltpu.VMEM((rows, 1), F32),
            pltpu.VMEM((rows, da), BF16),
            pltpu.VMEM((page, da), F32),
            pltpu.VMEM((page, da), F32),
        ],
    )
    return pl.pallas_call(
        functools.partial(_sample_attn_kernel, n_heads=n_heads, page=page, pages_per_step=pps),
        grid_spec=grid_spec,
        out_shape=jax.ShapeDtypeStruct((b, n_q, da), F32),
        compiler_params=_params("arbitrary", "arbitrary"),
        name="sample_attn",
    )(page_table, q, k_new, v_new, bias_rows, *([cache_k] * pps), *([cache_v] * pps))


def _row_tile(rows, target):
    return target if rows % target == 0 else rows


def _finish(x2d, o_attn, d, lw, tm):
    x1, h2 = _mix_out(x2d, o_attn, d, lw["pool_w"], lw["pool_scale"], lw["w_out"], lw["norm2_g"], tm)
    return _ffn(x1, h2, lw["w_gate"], lw["w_up"], lw["w_down"], tm, 512)


def kernel(x_prompt, x_sample, cache_k, cache_v, state_pool, page_table, norm1_g, w_in, q_norm_g, k_norm_g,
           sb_bias, pool_w, pool_scale, w_out, norm2_g, w_gate, w_up, w_down):
    b_p, seq, dm = x_prompt.shape
    b_s, dec_seq, _ = x_sample.shape
    depth = w_in.shape[0]
    n_pool_pages, page, n_heads, _ = cache_k.shape[1:]
    da = n_heads * HEAD_DIM
    dp = state_pool.shape[-1]
    past_len = page_table.shape[1] * page

    xp = x_prompt.reshape(b_p * seq, dm)
    xs = x_sample.reshape(b_s * dec_seq, dm)
    tm_p = _row_tile(b_p * seq, 512)
    tm_s = b_s * dec_seq
    outs = [[] for _ in range(6)]
    for l in range(depth):
        lw = {
            "pool_w": pool_w[l].astype(BF16),
            "pool_scale": pool_scale[l].reshape(1, dp),
            "w_out": w_out[l].astype(BF16),
            "norm2_g": norm2_g[l].reshape(1, dm),
            "w_gate": w_gate[l].astype(BF16),
            "w_up": w_up[l].astype(BF16),
            "w_down": w_down[l].astype(BF16),
        }
        g1 = norm1_g[l].reshape(1, dm)
        w_in_l = w_in[l].astype(BF16)
        qg = q_norm_g[l].reshape(1, HEAD_DIM)
        kg = k_norm_g[l].reshape(1, HEAD_DIM)

        q, k, v, p = _in_proj(xp, g1, w_in_l, qg, kg, tm_p)
        o_attn = _prompt_attn(q.reshape(b_p, seq, da), k.reshape(b_p, seq, da), v.reshape(b_p, seq, da),
                              sb_bias[l], 256)
        blocks_per_seq = seq // tm_p
        halo_per_block = tm_p // HALO
        d = _pool_delta(p.reshape(-1, HALO, dp), p.reshape(-1, tm_p, dp), tm_p, blocks_per_seq, 0,
                        lambda i: jnp.maximum(i * halo_per_block - 1, 0))
        outs[0].append(k.reshape(b_p, seq, n_heads, HEAD_DIM))
        outs[1].append(v.reshape(b_p, seq, n_heads, HEAD_DIM))
        outs[2].append(p.reshape(b_p, seq, dp)[:, seq - POOL_HIST:])
        xp = _finish(xp, o_attn.reshape(b_p * seq, da), d.reshape(b_p * seq, dp), lw, tm_p)

        q, k, v, p = _in_proj(xs, g1, w_in_l, qg, kg, tm_s)
        pad = 8 - dec_seq
        k_new = jnp.pad(k.reshape(b_s, dec_seq, da), ((0, 0), (0, pad), (0, 0)))
        v_new = jnp.pad(v.reshape(b_s, dec_seq, da), ((0, 0), (0, pad), (0, 0)))
        bias_rows = jnp.broadcast_to(jnp.tile(sb_bias[l].astype(F32), dec_seq)[:, None], (dec_seq * n_heads, page))
        o_attn = _sample_attn(q.astype(F32).reshape(b_s, dec_seq, da), k_new, v_new, bias_rows,
                              cache_k[l].reshape(n_pool_pages, page * n_heads, HEAD_DIM),
                              cache_v[l].reshape(n_pool_pages, page * n_heads, HEAD_DIM), page_table)
        p3 = p.reshape(b_s, dec_seq, dp)
        hist = jnp.pad(state_pool[l], ((0, 0), (HALO - POOL_HIST, 0), (0, 0)))
        d = _pool_delta(hist, jnp.pad(p3, ((0, 0), (0, pad), (0, 0))), 8, 1, past_len, lambda i: i)
        outs[3].append(k.reshape(b_s, dec_seq, n_heads, HEAD_DIM))
        outs[4].append(v.reshape(b_s, dec_seq, n_heads, HEAD_DIM))
        outs[5].append(jnp.concatenate([state_pool[l], p3], axis=1)[:, dec_seq:])
        xs = _finish(xs, o_attn.astype(BF16).reshape(b_s * dec_seq, da), d[:, :dec_seq].reshape(b_s * dec_seq, dp),
                     lw, tm_s)

    stack = lambda xs_: jnp.stack(xs_, axis=0)
    return (xp.reshape(b_p, seq, dm), xs.reshape(b_s, dec_seq, dm), stack(outs[0]), stack(outs[1]),
            stack(outs[2]), stack(outs[3]), stack(outs[4]), stack(outs[5]))
```

```python
import functools

import jax
import jax.numpy as jnp
from jax import lax
from jax.experimental import pallas as pl
from jax.experimental.pallas import tpu as pltpu

F32 = jnp.float32
BF16 = jnp.bfloat16

HEAD_DIM = 128
POOL_WINDOWS = (2, 4, 8, 16)
POOL_HIST = max(POOL_WINDOWS) - 1
HALO = 16
EPS = 1e-6
VMEM_LIMIT = 56 * 1024 * 1024
PAGES_PER_STEP = 8
PROMPT_UNROLL = 4


def _params(*sem):
    return pltpu.CompilerParams(dimension_semantics=sem, vmem_limit_bytes=VMEM_LIMIT)


LOG2E = 1.4426950408889634
SCORE_SCALE = HEAD_DIM ** -0.5 * LOG2E
MASKED_SCORE = -1e30
SOFTPLUS2_LINEAR = 60.0


def _softplus2(y):
    return jnp.maximum(y, 0.0) + jnp.log(1.0 + jnp.exp2(-jnp.abs(y))) * LOG2E


def _split_bf16(x):
    hi = x.astype(BF16)
    lo = (x - hi.astype(F32)).astype(BF16)
    return jnp.concatenate([hi, lo], axis=1)


def _minus_later_keys(n, page=None):
    j = lax.broadcasted_iota(jnp.int32, (n, n), 0)
    s = lax.broadcasted_iota(jnp.int32, (n, n), 1)
    if page is None:
        later = j > s
    else:
        later = (j // page < s // page) | ((j // page == s // page) & (j > s))
    m = jnp.where(later, -1.0, 0.0).astype(BF16)
    return jnp.concatenate([m, m], axis=0)


def _in_proj_kernel(x_ref, g1_ref, w_ref, qg_ref, kg_ref, q_ref, k_ref, v_ref, p_ref, h_ref):
    j = pl.program_id(1)

    @pl.when(j == 0)
    def _():
        x = x_ref[...]
        ms = jnp.mean(x * x, axis=-1, keepdims=True)
        h_ref[...] = (x * lax.rsqrt(ms + EPS) * g1_ref[...]).astype(BF16)

    n_chunk = w_ref.shape[1] // 256

    def project(c):
        return jnp.dot(h_ref[...], w_ref[:, c * 256:(c + 1) * 256], preferred_element_type=F32)

    def head_norm(u, g):
        outs = []
        for i in range(u.shape[1] // HEAD_DIM):
            blk = u[:, i * HEAD_DIM:(i + 1) * HEAD_DIM]
            ms = jnp.mean(blk * blk, axis=-1, keepdims=True)
            outs.append(blk * lax.rsqrt(ms + EPS) * g)
        return jnp.concatenate(outs, axis=1)

    @pl.when(j == 0)
    def _():
        for c in range(n_chunk):
            q_ref[:, c * 256:(c + 1) * 256] = (head_norm(project(c), qg_ref[...]) * SCORE_SCALE).astype(BF16)

    @pl.when(j == 1)
    def _():
        for c in range(n_chunk):
            k_ref[:, c * 256:(c + 1) * 256] = head_norm(project(c), kg_ref[...])

    @pl.when(j == 2)
    def _():
        for c in range(n_chunk):
            v_ref[:, c * 256:(c + 1) * 256] = project(c)

    @pl.when(j == 3)
    def _():
        for c in range(n_chunk):
            p_ref[:, c * 256:(c + 1) * 256] = project(c)


def _in_proj(x, g1, w_in, qg, kg, tm):
    rows, d = x.shape
    sec = w_in.shape[1] // 4
    out_spec = pl.BlockSpec((tm, sec), lambda i, j: (i, 0))
    return pl.pallas_call(
        _in_proj_kernel,
        grid=(rows // tm, 4),
        in_specs=[
            pl.BlockSpec((tm, d), lambda i, j: (i, 0)),
            pl.BlockSpec((1, d), lambda i, j: (0, 0)),
            pl.BlockSpec((d, sec), lambda i, j: (0, j)),
            pl.BlockSpec((1, HEAD_DIM), lambda i, j: (0, 0)),
            pl.BlockSpec((1, HEAD_DIM), lambda i, j: (0, 0)),
        ],
        out_specs=[out_spec, out_spec, out_spec, out_spec],
        out_shape=[
            jax.ShapeDtypeStruct((rows, sec), BF16),
            jax.ShapeDtypeStruct((rows, sec), F32),
            jax.ShapeDtypeStruct((rows, sec), F32),
            jax.ShapeDtypeStruct((rows, sec), F32),
        ],
        scratch_shapes=[pltpu.VMEM((tm, d), BF16)],
        compiler_params=_params("arbitrary", "arbitrary"),
        name="in_proj",
    )(x, g1, w_in, qg, kg)


def _prompt_block_tables(n_q_blocks, blk):
    blocks = [(qi, qi - n) for qi in range(n_q_blocks) for n in range(qi + 1)]
    pad = -(len(blocks) + 2) % PROMPT_UNROLL
    n_iter = len(blocks) + 2 + pad
    rows = [[0] * n_iter for _ in range(6)]
    for e in range(n_iter):
        qi, kj = blocks[min(max(e - pad, 0), len(blocks) - 1)]
        rows[0][e], rows[1][e], rows[2][e] = qi * blk, kj * blk, int(qi == kj)
        if e - pad - 2 >= 0:
            qi, kj = blocks[e - pad - 2]
            rows[3][e], rows[4][e], rows[5][e] = kj * blk, qi * blk, int(qi == kj)
    return jnp.asarray(rows, jnp.int32), n_iter


def _prompt_attn_kernel(tbl_ref, bias_ref, q_ref, k_ref, v_ref, o_ref, kb_ref, vb_ref, later_ref, bm_ref,
                        y_ref, w_ref, spent_ref, run_ref, acc_ref, *, blk, n_iter):
    h = pl.program_id(1)
    kb_ref[...] = k_ref[...].astype(BF16)
    vb_ref[...] = v_ref[...].astype(BF16)
    y_ref[...] = jnp.zeros(y_ref.shape, F32)
    w_ref[...] = jnp.zeros(w_ref.shape, F32)
    spent_ref[...] = jnp.zeros(spent_ref.shape, F32)
    run_ref[...] = jnp.zeros(run_ref.shape, F32)
    acc_ref[...] = jnp.zeros(acc_ref.shape, F32)
    later_ref[...] = _minus_later_keys(blk)[:blk]
    bias = jnp.full((blk, blk), bias_ref[h] * LOG2E, F32)
    visible = (lax.broadcasted_iota(jnp.int32, (blk, blk), 1) < lax.broadcasted_iota(jnp.int32, (blk, blk), 0))
    bm_ref[0] = bias
    bm_ref[1] = jnp.where(visible, bias, MASKED_SCORE)

    def iteration(e, par):
        q = q_ref[pl.ds(pl.multiple_of(tbl_ref[0, e], blk), blk), :]
        kb = kb_ref[pl.ds(pl.multiple_of(tbl_ref[1, e], blk), blk), :]
        y = lax.dot_general(q, kb, (((1,), (1,)), ((), ())), preferred_element_type=F32)
        y_ref[par] = y + bm_ref[tbl_ref[2, e]]

        y = y_ref[1 - par]
        sp = jnp.maximum(jnp.log(1.0 + jnp.exp2(jnp.minimum(y, SOFTPLUS2_LINEAR))) * LOG2E, y)
        spent_ref[1 - par] = jnp.sum(sp, axis=-1, keepdims=True)
        rem = jnp.dot(sp.astype(BF16), later_ref[...], preferred_element_type=F32)
        w_ref[1 - par] = y - sp + rem

        first = tbl_ref[5, e] == 1
        a = jnp.exp2(w_ref[par]).astype(BF16)
        vb = vb_ref[pl.ds(pl.multiple_of(tbl_ref[3, e], blk), blk), :]
        pv = jnp.dot(a, vb, preferred_element_type=F32)
        run = jnp.where(first, 0.0, run_ref[...])
        acc = jnp.where(first, 0.0, acc_ref[...]) + pv * jnp.exp2(run)
        run_ref[...] = run - spent_ref[par]
        acc_ref[...] = acc
        o_ref[pl.ds(pl.multiple_of(tbl_ref[4, e], blk), blk), :] = acc.astype(BF16)

    def unrolled(i, _):
        for u in range(PROMPT_UNROLL):
            iteration(PROMPT_UNROLL * i + u, u % 2)
        return 0

    lax.fori_loop(0, n_iter // PROMPT_UNROLL, unrolled, 0)


def _prompt_attn(q, k, v, sb_bias, blk):
    b, t, da = q.shape
    tables, n_iter = _prompt_block_tables(t // blk, blk)
    spec = pl.BlockSpec((None, t, HEAD_DIM), lambda bi, h, tbl: (bi, 0, h))
    grid_spec = pltpu.PrefetchScalarGridSpec(
        num_scalar_prefetch=1,
        grid=(b, da // HEAD_DIM),
        in_specs=[pl.BlockSpec(memory_space=pltpu.SMEM), spec, spec, spec],
        out_specs=spec,
        scratch_shapes=[
            pltpu.VMEM((t, HEAD_DIM), BF16),
            pltpu.VMEM((t, HEAD_DIM), BF16),
            pltpu.VMEM((blk, blk), BF16),
            pltpu.VMEM((2, blk, blk), F32),
            pltpu.VMEM((2, blk, blk), F32),
            pltpu.VMEM((2, blk, blk), F32),
            pltpu.VMEM((2, blk, 1), F32),
            pltpu.VMEM((blk, 1), F32),
            pltpu.VMEM((blk, HEAD_DIM), F32),
        ],
    )
    return pl.pallas_call(
        functools.partial(_prompt_attn_kernel, blk=blk, n_iter=n_iter),
        grid_spec=grid_spec,
        out_shape=jax.ShapeDtypeStruct((b, t, da), BF16),
        compiler_params=_params("arbitrary", "arbitrary"),
        name="prompt_attn",
    )(tables, sb_bias, q, k, v)


def _pool_delta_kernel(halo_ref, p_ref, d_ref, ext_ref, *, blocks_per_seq, base_pos):
    i = pl.program_id(0)
    per_step, tm, width = p_ref.shape
    group = width // len(POOL_WINDOWS)
    seq_blk = i % blocks_per_seq
    pos = base_pos + seq_blk * tm + lax.broadcasted_iota(jnp.int32, (tm, 1), 0)
    starts_empty = jnp.logical_and(seq_blk == 0, base_pos == 0)

    def one_block(s, _):
        ext_ref[pl.ds(HALO, tm), :] = p_ref[s]
        ext_ref[pl.ds(0, HALO), :] = jnp.where(starts_empty, 0.0, halo_ref[s])
        for g, w in enumerate(POOL_WINDOWS):
            cols = pl.ds(g * group, group)
            total = ext_ref[pl.ds(HALO, tm), cols]
            for back in range(1, w):
                total = total + ext_ref[pl.ds(HALO - back, tm), cols]
            cnt = jnp.minimum(pos + 1, w).astype(F32)
            d_ref[s, :, cols] = (total / cnt - ext_ref[pl.ds(HALO, tm), cols]).astype(BF16)
        return 0

    lax.fori_loop(0, per_step, one_block, 0)


def _pool_delta(halo_src, p, per_step, blocks_per_seq, base_pos, halo_index):
    n_blocks, tm, c = p.shape
    return pl.pallas_call(
        functools.partial(_pool_delta_kernel, blocks_per_seq=blocks_per_seq, base_pos=base_pos),
        grid=(n_blocks // per_step,),
        in_specs=[
            pl.BlockSpec((per_step, HALO, c), lambda i: (halo_index(i), 0, 0)),
            pl.BlockSpec((per_step, tm, c), lambda i: (i, 0, 0)),
        ],
        out_specs=pl.BlockSpec((per_step, tm, c), lambda i: (i, 0, 0)),
        out_shape=jax.ShapeDtypeStruct((n_blocks, tm, c), BF16),
        scratch_shapes=[pltpu.VMEM((HALO + tm, c), F32)],
        compiler_params=_params("arbitrary"),
        name="pool_delta",
    )(halo_src, p)


def _mix_out_kernel(x_ref, oa_ref, d_ref, pw_ref, ps_ref, wo_ref, g2_ref, x1_ref, h2_ref):
    da = oa_ref.shape[1]
    n_groups, group, _ = pw_ref.shape
    pooled = []
    for g in range(n_groups):
        y = jnp.dot(d_ref[:, g * group:(g + 1) * group], pw_ref[g], preferred_element_type=F32)
        pooled.append((y * ps_ref[:, g * group:(g + 1) * group]).astype(BF16))
    o_pool = jnp.concatenate(pooled, axis=1)
    y = jnp.dot(oa_ref[...], wo_ref[:da, :], preferred_element_type=F32)
    y = y + jnp.dot(o_pool, wo_ref[da:, :], preferred_element_type=F32)
    x1 = x_ref[...] + y
    x1_ref[...] = x1
    ms = jnp.mean(x1 * x1, axis=-1, keepdims=True)
    h2_ref[...] = (x1 * lax.rsqrt(ms + EPS) * g2_ref[...]).astype(BF16)


def _mix_out(x, oa, d, pool_w, pool_scale, w_out, g2, tm):
    rows, dm = x.shape
    da = oa.shape[1]
    dp = d.shape[1]
    row = lambda i: (i, 0)
    fixed2 = lambda i: (0, 0)
    return pl.pallas_call(
        _mix_out_kernel,
        grid=(rows // tm,),
        in_specs=[
            pl.BlockSpec((tm, dm), row),
            pl.BlockSpec((tm, da), row),
            pl.BlockSpec((tm, dp), row),
            pl.BlockSpec(pool_w.shape, lambda i: (0, 0, 0)),
            pl.BlockSpec((1, dp), fixed2),
            pl.BlockSpec(w_out.shape, fixed2),
            pl.BlockSpec((1, dm), fixed2),
        ],
        out_specs=[pl.BlockSpec((tm, dm), row), pl.BlockSpec((tm, dm), row)],
        out_shape=[jax.ShapeDtypeStruct((rows, dm), F32), jax.ShapeDtypeStruct((rows, dm), BF16)],
        compiler_params=_params("arbitrary"),
        name="mix_out",
    )(x, oa, d, pool_w, pool_scale, w_out, g2)


def _ffn_kernel(x1_ref, h2_ref, wg_ref, wu_ref, wd_ref, o_ref):
    f = pl.program_id(1)

    @pl.when(f == 0)
    def _():
        o_ref[...] = x1_ref[...]

    h2 = h2_ref[...]
    g = jnp.dot(h2, wg_ref[...], preferred_element_type=F32)
    u = jnp.dot(h2, wu_ref[...], preferred_element_type=F32)
    ff = (g / (1.0 + jnp.exp(-g))) * u
    o_ref[...] += jnp.dot(ff.astype(BF16), wd_ref[...], preferred_element_type=F32)


def _ffn(x1, h2, wg, wu, wd, tm, tf):
    rows, dm = x1.shape
    dff = wg.shape[1]
    return pl.pallas_call(
        _ffn_kernel,
        grid=(rows // tm, dff // tf),
        in_specs=[
            pl.BlockSpec((tm, dm), lambda i, f: (i, 0)),
            pl.BlockSpec((tm, dm), lambda i, f: (i, 0)),
            pl.BlockSpec((dm, tf), lambda i, f: (0, f)),
            pl.BlockSpec((dm, tf), lambda i, f: (0, f)),
            pl.BlockSpec((tf, dm), lambda i, f: (f, 0)),
        ],
        out_specs=pl.BlockSpec((tm, dm), lambda i, f: (i, 0)),
        out_shape=jax.ShapeDtypeStruct((rows, dm), F32),
        compiler_params=_params("arbitrary", "arbitrary"),
        name="ffn",
    )(x1, h2, wg, wu, wd)


def _sample_attn_kernel(pt_ref, q_ref, kn_ref, vn_ref, bias_ref, *refs, n_heads, page, pages_per_step):
    del pt_ref
    k_refs = refs[:pages_per_step]
    v_refs = refs[pages_per_step:2 * pages_per_step]
    o_ref, acc_ref, run_ref, qbd_ref, kpad_ref, vpad_ref = refs[2 * pages_per_step:]
    j = pl.program_id(1)
    n_q = q_ref.shape[0]
    rows = n_q * n_heads
    da = n_heads * HEAD_DIM
    group = 2 * page
    n_groups = pages_per_step // 2
    head_of_row = lax.broadcasted_iota(jnp.int32, (rows, da), 0) % n_heads
    head_of_col = lax.broadcasted_iota(jnp.int32, (rows, da), 1) // HEAD_DIM
    own_head = head_of_row == head_of_col
    bias = bias_ref[...] * LOG2E

    def scores(keys, n_tiles):
        y = lax.dot_general(qbd_ref[...], keys, (((1,), (1,)), ((), ())), preferred_element_type=F32)
        return y + jnp.concatenate([bias] * n_tiles, axis=1)

    def heads_side_by_side(ref):
        return jnp.concatenate(
            [ref[pl.ds(h, page, stride=n_heads), :] for h in range(n_heads)], axis=1).astype(BF16)

    @pl.when(j == 0)
    def _():
        q = q_ref[...]
        q_rows = jnp.concatenate(
            [jnp.broadcast_to(q[t:t + 1, :], (n_heads, da)) for t in range(n_q)], axis=0)
        qbd_ref[...] = jnp.where(own_head, q_rows, 0.0).astype(BF16)
        n_new = kn_ref.shape[0]
        kpad_ref[...] = jnp.zeros(kpad_ref.shape, F32)
        vpad_ref[...] = jnp.zeros(vpad_ref.shape, F32)
        kpad_ref[pl.ds(0, n_new), :] = kn_ref[...]
        vpad_ref[pl.ds(0, n_new), :] = vn_ref[...]
        t_of_row = lax.broadcasted_iota(jnp.int32, (rows, page), 0) // n_heads
        key = lax.broadcasted_iota(jnp.int32, (rows, page), 1)
        causal = key < t_of_row
        y = scores(kpad_ref[...].astype(BF16), 1)
        sp = jnp.where(causal, _softplus2(y), 0.0)
        rem = jnp.dot(_split_bf16(sp), _minus_later_keys(page), preferred_element_type=F32)
        a = jnp.where(causal, jnp.exp2(y - sp + rem), 0.0)
        acc_ref[...] = jnp.dot(a.astype(BF16), vpad_ref[...].astype(BF16), preferred_element_type=F32)
        run_ref[...] = -jnp.sum(sp, axis=-1, keepdims=True)

    keys = jnp.concatenate([heads_side_by_side(r) for r in k_refs], axis=0)
    y = scores(keys, pages_per_step)
    sp = _softplus2(y)
    split = jnp.concatenate(
        [_split_bf16(sp[:, g * group:(g + 1) * group]) for g in range(n_groups)], axis=0)
    rem = jnp.dot(split, _minus_later_keys(group, page), preferred_element_type=F32)
    run = run_ref[...]
    weights = []
    for g in range(n_groups):
        cols = slice(g * group, (g + 1) * group)
        a = jnp.exp2(y[:, cols] - sp[:, cols] + rem[g * rows:(g + 1) * rows]) * jnp.exp2(run)
        weights.append(a.astype(BF16))
        run = run - jnp.sum(sp[:, cols], axis=-1, keepdims=True)
    run_ref[...] = run
    values = jnp.concatenate([heads_side_by_side(r) for r in v_refs], axis=0)
    acc = acc_ref[...] + jnp.dot(jnp.concatenate(weights, axis=1), values, preferred_element_type=F32)
    acc_ref[...] = acc

    @pl.when(j == pl.num_programs(1) - 1)
    def _():
        kept = jnp.where(own_head, acc, 0.0)
        o_ref[...] = jnp.sum(kept.reshape(n_q, n_heads, da), axis=1)


def _sample_attn(q, k_new, v_new, bias_rows, cache_k, cache_v, page_table):
    b, n_q, da = q.shape
    n_heads = da // HEAD_DIM
    rows = n_q * n_heads
    n_pages = page_table.shape[1]
    page = cache_k.shape[1] // n_heads
    pps = PAGES_PER_STEP
    steps = n_pages // pps

    def page_spec(i):
        return pl.BlockSpec(
            (None, page * n_heads, HEAD_DIM),
            lambda bi, j, pt, i=i: (pt[bi, n_pages - 1 - (j * pps + i)], 0, 0))

    per_batch = lambda bi, j, pt: (bi, 0, 0)
    grid_spec = pltpu.PrefetchScalarGridSpec(
        num_scalar_prefetch=1,
        grid=(b, steps),
        in_specs=[
            pl.BlockSpec((None, n_q, da), per_batch),
            pl.BlockSpec((None, k_new.shape[1], da), per_batch),
            pl.BlockSpec((None, v_new.shape[1], da), per_batch),
            pl.BlockSpec((rows, page), lambda bi, j, pt: (0, 0)),
        ] + [page_spec(i) for i in range(pps)] * 2,
        out_specs=pl.BlockSpec((None, n_q, da), per_batch),
        scratch_shapes=[
            pltpu.VMEM((rows, da), F32),
            pltpu.VMEM((rows, 1), F32),
            pltpu.VMEM((rows, da), BF16),
            pltpu.VMEM((page, da), F32),
            pltpu.VMEM((page, da), F32),
        ],
    )
    return pl.pallas_call(
        functools.partial(_sample_attn_kernel, n_heads=n_heads, page=page, pages_per_step=pps),
        grid_spec=grid_spec,
        out_shape=jax.ShapeDtypeStruct((b, n_q, da), F32),
        compiler_params=_params("arbitrary", "arbitrary"),
        name="sample_attn",
    )(page_table, q, k_new, v_new, bias_rows, *([cache_k] * pps), *([cache_v] * pps))


def _row_tile(rows, target):
    return target if rows % target == 0 else rows


def _finish(x2d, o_attn, d, lw, tm):
    x1, h2 = _mix_out(x2d, o_attn, d, lw["pool_w"], lw["pool_scale"], lw["w_out"], lw["norm2_g"], tm)
    return _ffn(x1, h2, lw["w_gate"], lw["w_up"], lw["w_down"], tm, 512)


def kernel(x_prompt, x_sample, cache_k, cache_v, state_pool, page_table, norm1_g, w_in, q_norm_g, k_norm_g,
           sb_bias, pool_w, pool_scale, w_out, norm2_g, w_gate, w_up, w_down):
    b_p, seq, dm = x_prompt.shape
    b_s, dec_seq, _ = x_sample.shape
    depth = w_in.shape[0]
    n_pool_pages, page, n_heads, _ = cache_k.shape[1:]
    da = n_heads * HEAD_DIM
    dp = state_pool.shape[-1]
    past_len = page_table.shape[1] * page

    xp = x_prompt.reshape(b_p * seq, dm)
    xs = x_sample.reshape(b_s * dec_seq, dm)
    tm_p = _row_tile(b_p * seq, 512)
    tm_s = b_s * dec_seq
    outs = [[] for _ in range(6)]
    for l in range(depth):
        lw = {
            "pool_w": pool_w[l].astype(BF16),
            "pool_scale": pool_scale[l].reshape(1, dp),
            "w_out": w_out[l].astype(BF16),
            "norm2_g": norm2_g[l].reshape(1, dm),
            "w_gate": w_gate[l].astype(BF16),
            "w_up": w_up[l].astype(BF16),
            "w_down": w_down[l].astype(BF16),
        }
        g1 = norm1_g[l].reshape(1, dm)
        w_in_l = w_in[l].astype(BF16)
        qg = q_norm_g[l].reshape(1, HEAD_DIM)
        kg = k_norm_g[l].reshape(1, HEAD_DIM)

        q, k, v, p = _in_proj(xp, g1, w_in_l, qg, kg, tm_p)
        o_attn = _prompt_attn(q.reshape(b_p, seq, da), k.reshape(b_p, seq, da), v.reshape(b_p, seq, da),
                              sb_bias[l], 256)
        blocks_per_seq = seq // tm_p
        halo_per_block = tm_p // HALO
        d = _pool_delta(p.reshape(-1, HALO, dp), p.reshape(-1, tm_p, dp), 1, blocks_per_seq, 0,
                        lambda i: jnp.maximum(i * halo_per_block - 1, 0))
        outs[0].append(k.reshape(b_p, seq, n_heads, HEAD_DIM))
        outs[1].append(v.reshape(b_p, seq, n_heads, HEAD_DIM))
        outs[2].append(p.reshape(b_p, seq, dp)[:, seq - POOL_HIST:])
        xp = _finish(xp, o_attn.reshape(b_p * seq, da), d.reshape(b_p * seq, dp), lw, tm_p)

        q, k, v, p = _in_proj(xs, g1, w_in_l, qg, kg, tm_s)
        pad = 8 - dec_seq
        k_new = jnp.pad(k.reshape(b_s, dec_seq, da), ((0, 0), (0, pad), (0, 0)))
        v_new = jnp.pad(v.reshape(b_s, dec_seq, da), ((0, 0), (0, pad), (0, 0)))
        bias_rows = jnp.broadcast_to(jnp.tile(sb_bias[l].astype(F32), dec_seq)[:, None], (dec_seq * n_heads, page))
        o_attn = _sample_attn(q.astype(F32).reshape(b_s, dec_seq, da), k_new, v_new, bias_rows,
                              cache_k[l].reshape(n_pool_pages, page * n_heads, HEAD_DIM),
                              cache_v[l].reshape(n_pool_pages, page * n_heads, HEAD_DIM), page_table)
        p3 = p.reshape(b_s, dec_seq, dp)
        hist = jnp.pad(state_pool[l], ((0, 0), (HALO - POOL_HIST, 0), (0, 0)))
        d = _pool_delta(hist, jnp.pad(p3, ((0, 0), (0, pad), (0, 0))), b_s, 1, past_len, lambda i: i)
        outs[3].append(k.reshape(b_s, dec_seq, n_heads, HEAD_DIM))
        outs[4].append(v.reshape(b_s, dec_seq, n_heads, HEAD_DIM))
        outs[5].append(jnp.concatenate([state_pool[l], p3], axis=1)[:, dec_seq:])
        xs = _finish(xs, o_attn.astype(BF16).reshape(b_s * dec_seq, da), d[:, :dec_seq].reshape(b_s * dec_seq, dp),
                     lw, tm_s)

    stack = lambda xs_: jnp.stack(xs_, axis=0)
    return (xp.reshape(b_p, seq, dm), xs.reshape(b_s, dec_seq, dm), stack(outs[0]), stack(outs[1]),
            stack(outs[2]), stack(outs[3]), stack(outs[4]), stack(outs[5]))
```

```python
import functools

import jax
import jax.numpy as jnp
from jax import lax
from jax.experimental import pallas as pl
from jax.experimental.pallas import tpu as pltpu

F32 = jnp.float32
BF16 = jnp.bfloat16

HEAD_DIM = 128
POOL_WINDOWS = (2, 4, 8, 16)
POOL_HIST = max(POOL_WINDOWS) - 1
HALO = 16
EPS = 1e-6
VMEM_LIMIT = 56 * 1024 * 1024
PAGES_PER_STEP = 16
PROMPT_UNROLL = 4


def _params(*sem):
    return pltpu.CompilerParams(dimension_semantics=sem, vmem_limit_bytes=VMEM_LIMIT)


LOG2E = 1.4426950408889634
SCORE_SCALE = HEAD_DIM ** -0.5 * LOG2E
MASKED_SCORE = -1e30
SOFTPLUS2_LINEAR = 60.0


def _softplus2(y):
    return jnp.maximum(y, 0.0) + jnp.log(1.0 + jnp.exp2(-jnp.abs(y))) * LOG2E


def _split_bf16(x):
    hi = x.astype(BF16)
    lo = (x - hi.astype(F32)).astype(BF16)
    return jnp.concatenate([hi, lo], axis=1)


def _minus_later_keys(n, page=None):
    j = lax.broadcasted_iota(jnp.int32, (n, n), 0)
    s = lax.broadcasted_iota(jnp.int32, (n, n), 1)
    if page is None:
        later = j > s
    else:
        later = (j // page < s // page) | ((j // page == s // page) & (j > s))
    m = jnp.where(later, -1.0, 0.0).astype(BF16)
    return jnp.concatenate([m, m], axis=0)


def _in_proj_kernel(x_ref, g1_ref, w_ref, qg_ref, kg_ref, q_ref, k_ref, v_ref, p_ref, h_ref):
    j = pl.program_id(1)

    @pl.when(j == 0)
    def _():
        x = x_ref[...]
        ms = jnp.mean(x * x, axis=-1, keepdims=True)
        h_ref[...] = (x * lax.rsqrt(ms + EPS) * g1_ref[...]).astype(BF16)

    n_chunk = w_ref.shape[1] // 256

    def project(c):
        return jnp.dot(h_ref[...], w_ref[:, c * 256:(c + 1) * 256], preferred_element_type=F32)

    def head_norm(u, g):
        outs = []
        for i in range(u.shape[1] // HEAD_DIM):
            blk = u[:, i * HEAD_DIM:(i + 1) * HEAD_DIM]
            ms = jnp.mean(blk * blk, axis=-1, keepdims=True)
            outs.append(blk * lax.rsqrt(ms + EPS) * g)
        return jnp.concatenate(outs, axis=1)

    @pl.when(j == 0)
    def _():
        for c in range(n_chunk):
            q_ref[:, c * 256:(c + 1) * 256] = (head_norm(project(c), qg_ref[...]) * SCORE_SCALE).astype(BF16)

    @pl.when(j == 1)
    def _():
        for c in range(n_chunk):
            k_ref[:, c * 256:(c + 1) * 256] = head_norm(project(c), kg_ref[...])

    @pl.when(j == 2)
    def _():
        for c in range(n_chunk):
            v_ref[:, c * 256:(c + 1) * 256] = project(c)

    @pl.when(j == 3)
    def _():
        for c in range(n_chunk):
            p_ref[:, c * 256:(c + 1) * 256] = project(c)


def _in_proj(x, g1, w_in, qg, kg, tm):
    rows, d = x.shape
    sec = w_in.shape[1] // 4
    out_spec = pl.BlockSpec((tm, sec), lambda i, j: (i, 0))
    return pl.pallas_call(
        _in_proj_kernel,
        grid=(rows // tm, 4),
        in_specs=[
            pl.BlockSpec((tm, d), lambda i, j: (i, 0)),
            pl.BlockSpec((1, d), lambda i, j: (0, 0)),
            pl.BlockSpec((d, sec), lambda i, j: (0, j)),
            pl.BlockSpec((1, HEAD_DIM), lambda i, j: (0, 0)),
            pl.BlockSpec((1, HEAD_DIM), lambda i, j: (0, 0)),
        ],
        out_specs=[out_spec, out_spec, out_spec, out_spec],
        out_shape=[
            jax.ShapeDtypeStruct((rows, sec), BF16),
            jax.ShapeDtypeStruct((rows, sec), F32),
            jax.ShapeDtypeStruct((rows, sec), F32),
            jax.ShapeDtypeStruct((rows, sec), F32),
        ],
        scratch_shapes=[pltpu.VMEM((tm, d), BF16)],
        compiler_params=_params("arbitrary", "arbitrary"),
        name="in_proj",
    )(x, g1, w_in, qg, kg)


def _prompt_block_tables(n_q_blocks, blk):
    blocks = [(qi, qi - n) for qi in range(n_q_blocks) for n in range(qi + 1)]
    pad = -(len(blocks) + 2) % PROMPT_UNROLL
    n_iter = len(blocks) + 2 + pad
    rows = [[0] * n_iter for _ in range(6)]
    for e in range(n_iter):
        qi, kj = blocks[min(max(e - pad, 0), len(blocks) - 1)]
        rows[0][e], rows[1][e], rows[2][e] = qi * blk, kj * blk, int(qi == kj)
        if e - pad - 2 >= 0:
            qi, kj = blocks[e - pad - 2]
            rows[3][e], rows[4][e], rows[5][e] = kj * blk, qi * blk, int(qi == kj)
    return jnp.asarray(rows, jnp.int32), n_iter


def _prompt_attn_kernel(tbl_ref, bias_ref, q_ref, k_ref, v_ref, o_ref, kb_ref, vb_ref, later_ref, bm_ref,
                        y_ref, w_ref, spent_ref, run_ref, acc_ref, *, blk, n_iter):
    h = pl.program_id(1)
    kb_ref[...] = k_ref[...].astype(BF16)
    vb_ref[...] = v_ref[...].astype(BF16)
    y_ref[...] = jnp.zeros(y_ref.shape, F32)
    w_ref[...] = jnp.zeros(w_ref.shape, F32)
    spent_ref[...] = jnp.zeros(spent_ref.shape, F32)
    run_ref[...] = jnp.zeros(run_ref.shape, F32)
    acc_ref[...] = jnp.zeros(acc_ref.shape, F32)
    later_ref[...] = _minus_later_keys(blk)[:blk]
    bias = jnp.full((blk, blk), bias_ref[h] * LOG2E, F32)
    visible = (lax.broadcasted_iota(jnp.int32, (blk, blk), 1) < lax.broadcasted_iota(jnp.int32, (blk, blk), 0))
    bm_ref[0] = bias
    bm_ref[1] = jnp.where(visible, bias, MASKED_SCORE)

    def iteration(e, par):
        q = q_ref[pl.ds(pl.multiple_of(tbl_ref[0, e], blk), blk), :]
        kb = kb_ref[pl.ds(pl.multiple_of(tbl_ref[1, e], blk), blk), :]
        y = lax.dot_general(q, kb, (((1,), (1,)), ((), ())), preferred_element_type=F32)
        y_ref[par] = y + bm_ref[tbl_ref[2, e]]

        y = y_ref[1 - par]
        sp = jnp.maximum(jnp.log(1.0 + jnp.exp2(jnp.minimum(y, SOFTPLUS2_LINEAR))) * LOG2E, y)
        spent_ref[1 - par] = jnp.sum(sp, axis=-1, keepdims=True)
        rem = jnp.dot(sp.astype(BF16), later_ref[...], preferred_element_type=F32)
        w_ref[1 - par] = y - sp + rem

        first = tbl_ref[5, e] == 1
        a = jnp.exp2(w_ref[par]).astype(BF16)
        vb = vb_ref[pl.ds(pl.multiple_of(tbl_ref[3, e], blk), blk), :]
        pv = jnp.dot(a, vb, preferred_element_type=F32)
        run = jnp.where(first, 0.0, run_ref[...])
        acc = jnp.where(first, 0.0, acc_ref[...]) + pv * jnp.exp2(run)
        run_ref[...] = run - spent_ref[par]
        acc_ref[...] = acc
        o_ref[pl.ds(pl.multiple_of(tbl_ref[4, e], blk), blk), :] = acc.astype(BF16)

    def unrolled(i, _):
        for u in range(PROMPT_UNROLL):
            iteration(PROMPT_UNROLL * i + u, u % 2)
        return 0

    lax.fori_loop(0, n_iter // PROMPT_UNROLL, unrolled, 0)


def _prompt_attn(q, k, v, sb_bias, blk):
    b, t, da = q.shape
    tables, n_iter = _prompt_block_tables(t // blk, blk)
    spec = pl.BlockSpec((None, t, HEAD_DIM), lambda bi, h, tbl: (bi, 0, h))
    grid_spec = pltpu.PrefetchScalarGridSpec(
        num_scalar_prefetch=1,
        grid=(b, da // HEAD_DIM),
        in_specs=[pl.BlockSpec(memory_space=pltpu.SMEM), spec, spec, spec],
        out_specs=spec,
        scratch_shapes=[
            pltpu.VMEM((t, HEAD_DIM), BF16),
            pltpu.VMEM((t, HEAD_DIM), BF16),
            pltpu.VMEM((blk, blk), BF16),
            pltpu.VMEM((2, blk, blk), F32),
            pltpu.VMEM((2, blk, blk), F32),
            pltpu.VMEM((2, blk, blk), F32),
            pltpu.VMEM((2, blk, 1), F32),
            pltpu.VMEM((blk, 1), F32),
            pltpu.VMEM((blk, HEAD_DIM), F32),
        ],
    )
    return pl.pallas_call(
        functools.partial(_prompt_attn_kernel, blk=blk, n_iter=n_iter),
        grid_spec=grid_spec,
        out_shape=jax.ShapeDtypeStruct((b, t, da), BF16),
        compiler_params=_params("arbitrary", "arbitrary"),
        name="prompt_attn",
    )(tables, sb_bias, q, k, v)


def _window_deltas(ext_ref, tm, pos):
    group = ext_ref.shape[1] // len(POOL_WINDOWS)
    deltas = []
    for g, w in enumerate(POOL_WINDOWS):
        rows = ext_ref[:, pl.ds(g * group, group)]
        total, shift = rows, 1
        while shift < w:
            total = total + pltpu.roll(total, shift, axis=0)
            shift *= 2
        cnt = jnp.minimum(pos + 1, w).astype(F32)
        deltas.append((total[HALO:] / cnt - rows[HALO:]).astype(BF16))
    return deltas


def _pool_delta_kernel(halo_ref, p_ref, d_ref, ext_ref, *, base_pos):
    n_seq, tm, width = p_ref.shape
    group = width // len(POOL_WINDOWS)
    pos = base_pos + lax.broadcasted_iota(jnp.int32, (tm, 1), 0)

    def one_sequence(s, _):
        ext_ref[pl.ds(0, HALO), :] = halo_ref[s]
        ext_ref[pl.ds(HALO, tm), :] = p_ref[s]
        for g, delta in enumerate(_window_deltas(ext_ref, tm, pos)):
            d_ref[s, :, pl.ds(g * group, group)] = delta
        return 0

    lax.fori_loop(0, n_seq, one_sequence, 0)


def _pool_delta(history, p, base_pos):
    n_seq, tm, c = p.shape
    whole = lambda i: (0, 0, 0)
    return pl.pallas_call(
        functools.partial(_pool_delta_kernel, base_pos=base_pos),
        grid=(1,),
        in_specs=[pl.BlockSpec((n_seq, HALO, c), whole), pl.BlockSpec((n_seq, tm, c), whole)],
        out_specs=pl.BlockSpec((n_seq, tm, c), whole),
        out_shape=jax.ShapeDtypeStruct((n_seq, tm, c), BF16),
        scratch_shapes=[pltpu.VMEM((HALO + tm, c), F32)],
        compiler_params=_params("arbitrary"),
        name="pool_delta",
    )(history, p)


def _mix_finish(x_ref, oa_ref, deltas, pw_ref, ps_ref, wo_ref, g2_ref, x1_ref, h2_ref):
    da = oa_ref.shape[1]
    group = pw_ref.shape[1]
    y = jnp.dot(oa_ref[...], wo_ref[:da, :], preferred_element_type=F32)
    pooled = []
    for g, delta in enumerate(deltas):
        y_g = jnp.dot(delta, pw_ref[g], preferred_element_type=F32)
        pooled.append((y_g * ps_ref[:, g * group:(g + 1) * group]).astype(BF16))
    y = y + jnp.dot(jnp.concatenate(pooled, axis=1), wo_ref[da:, :], preferred_element_type=F32)
    x1 = x_ref[...] + y
    x1_ref[...] = x1
    ms = jnp.mean(x1 * x1, axis=-1, keepdims=True)
    h2_ref[...] = (x1 * lax.rsqrt(ms + EPS) * g2_ref[...]).astype(BF16)


def _mix_out_kernel(x_ref, oa_ref, d_ref, pw_ref, ps_ref, wo_ref, g2_ref, x1_ref, h2_ref):
    group = pw_ref.shape[1]
    deltas = [d_ref[:, g * group:(g + 1) * group] for g in range(pw_ref.shape[0])]
    _mix_finish(x_ref, oa_ref, deltas, pw_ref, ps_ref, wo_ref, g2_ref, x1_ref, h2_ref)


def _mix_out_pool_kernel(x_ref, oa_ref, halo_ref, p_ref, pw_ref, ps_ref, wo_ref, g2_ref, x1_ref, h2_ref,
                         ext_ref, *, blocks_per_seq):
    tm = p_ref.shape[0]
    seq_blk = pl.program_id(0) % blocks_per_seq
    ext_ref[pl.ds(0, HALO), :] = jnp.where(seq_blk == 0, 0.0, halo_ref[...])
    ext_ref[pl.ds(HALO, tm), :] = p_ref[...]
    pos = seq_blk * tm + lax.broadcasted_iota(jnp.int32, (tm, 1), 0)
    deltas = _window_deltas(ext_ref, tm, pos)
    _mix_finish(x_ref, oa_ref, deltas, pw_ref, ps_ref, wo_ref, g2_ref, x1_ref, h2_ref)


def _mix_out(x, oa, pool_in, pool_w, pool_scale, w_out, g2, tm, blocks_per_seq=None):
    rows, dm = x.shape
    da = oa.shape[1]
    dp = pool_in.shape[1]
    row = lambda i: (i, 0)
    fixed2 = lambda i: (0, 0)
    weight_specs = [
        pl.BlockSpec(pool_w.shape, lambda i: (0, 0, 0)),
        pl.BlockSpec((1, dp), fixed2),
        pl.BlockSpec(w_out.shape, fixed2),
        pl.BlockSpec((1, dm), fixed2),
    ]
    common = dict(
        grid=(rows // tm,),
        out_specs=[pl.BlockSpec((tm, dm), row), pl.BlockSpec((tm, dm), row)],
        out_shape=[jax.ShapeDtypeStruct((rows, dm), F32), jax.ShapeDtypeStruct((rows, dm), BF16)],
        compiler_params=_params("arbitrary"),
        name="mix_out",
    )
    if blocks_per_seq is None:
        return pl.pallas_call(
            _mix_out_kernel,
            in_specs=[pl.BlockSpec((tm, dm), row), pl.BlockSpec((tm, da), row), pl.BlockSpec((tm, dp), row)]
            + weight_specs,
            **common,
        )(x, oa, pool_in, pool_w, pool_scale, w_out, g2)
    halo_per_block = tm // HALO
    return pl.pallas_call(
        functools.partial(_mix_out_pool_kernel, blocks_per_seq=blocks_per_seq),
        in_specs=[
            pl.BlockSpec((tm, dm), row),
            pl.BlockSpec((tm, da), row),
            pl.BlockSpec((None, HALO, dp), lambda i: (jnp.maximum(i * halo_per_block - 1, 0), 0, 0)),
            pl.BlockSpec((tm, dp), row),
        ] + weight_specs,
        scratch_shapes=[pltpu.VMEM((HALO + tm, dp), F32)],
        **common,
    )(x, oa, pool_in.reshape(rows // HALO, HALO, dp), pool_in, pool_w, pool_scale, w_out, g2)


def _ffn_kernel(x1_ref, h2_ref, wg_ref, wu_ref, wd_ref, o_ref):
    f = pl.program_id(1)

    @pl.when(f == 0)
    def _():
        o_ref[...] = x1_ref[...]

    h2 = h2_ref[...]
    g = jnp.dot(h2, wg_ref[...], preferred_element_type=F32)
    u = jnp.dot(h2, wu_ref[...], preferred_element_type=F32)
    ff = (g / (1.0 + jnp.exp(-g))) * u
    o_ref[...] += jnp.dot(ff.astype(BF16), wd_ref[...], preferred_element_type=F32)


def _ffn(x1, h2, wg, wu, wd, tm, tf):
    rows, dm = x1.shape
    dff = wg.shape[1]
    return pl.pallas_call(
        _ffn_kernel,
        grid=(rows // tm, dff // tf),
        in_specs=[
            pl.BlockSpec((tm, dm), lambda i, f: (i, 0)),
            pl.BlockSpec((tm, dm), lambda i, f: (i, 0)),
            pl.BlockSpec((dm, tf), lambda i, f: (0, f)),
            pl.BlockSpec((dm, tf), lambda i, f: (0, f)),
            pl.BlockSpec((tf, dm), lambda i, f: (f, 0)),
        ],
        out_specs=pl.BlockSpec((tm, dm), lambda i, f: (i, 0)),
        out_shape=jax.ShapeDtypeStruct((rows, dm), F32),
        compiler_params=_params("arbitrary", "arbitrary"),
        name="ffn",
    )(x1, h2, wg, wu, wd)


def _sample_attn_kernel(pt_ref, q_ref, kn_ref, vn_ref, bias_ref, *refs, n_heads, page, pages_per_step):
    del pt_ref
    k_refs = refs[:pages_per_step]
    v_refs = refs[pages_per_step:2 * pages_per_step]
    o_ref, acc_ref, run_ref, qbd_ref, kpad_ref, vpad_ref = refs[2 * pages_per_step:]
    j = pl.program_id(1)
    n_q = q_ref.shape[0]
    rows = n_q * n_heads
    da = n_heads * HEAD_DIM
    group = 2 * page
    n_groups = pages_per_step // 2
    head_of_row = lax.broadcasted_iota(jnp.int32, (rows, da), 0) % n_heads
    head_of_col = lax.broadcasted_iota(jnp.int32, (rows, da), 1) // HEAD_DIM
    own_head = head_of_row == head_of_col
    bias = bias_ref[...] * LOG2E

    def scores(keys, n_tiles):
        y = lax.dot_general(qbd_ref[...], keys, (((1,), (1,)), ((), ())), preferred_element_type=F32)
        return y + jnp.concatenate([bias] * n_tiles, axis=1)

    def heads_side_by_side(ref):
        return jnp.concatenate(
            [ref[pl.ds(h, page, stride=n_heads), :] for h in range(n_heads)], axis=1).astype(BF16)

    @pl.when(j == 0)
    def _():
        q = q_ref[...]
        q_rows = jnp.concatenate(
            [jnp.broadcast_to(q[t:t + 1, :], (n_heads, da)) for t in range(n_q)], axis=0)
        qbd_ref[...] = jnp.where(own_head, q_rows, 0.0).astype(BF16)
        n_new = kn_ref.shape[0]
        kpad_ref[...] = jnp.zeros(kpad_ref.shape, F32)
        vpad_ref[...] = jnp.zeros(vpad_ref.shape, F32)
        kpad_ref[pl.ds(0, n_new), :] = kn_ref[...]
        vpad_ref[pl.ds(0, n_new), :] = vn_ref[...]
        t_of_row = lax.broadcasted_iota(jnp.int32, (rows, page), 0) // n_heads
        key = lax.broadcasted_iota(jnp.int32, (rows, page), 1)
        causal = key < t_of_row
        y = scores(kpad_ref[...].astype(BF16), 1)
        sp = jnp.where(causal, _softplus2(y), 0.0)
        rem = jnp.dot(_split_bf16(sp), _minus_later_keys(page), preferred_element_type=F32)
        a = jnp.where(causal, jnp.exp2(y - sp + rem), 0.0)
        acc_ref[...] = jnp.dot(a.astype(BF16), vpad_ref[...].astype(BF16), preferred_element_type=F32)
        run_ref[...] = -jnp.sum(sp, axis=-1, keepdims=True)

    keys = jnp.concatenate([heads_side_by_side(r) for r in k_refs], axis=0)
    y = scores(keys, pages_per_step)
    sp = _softplus2(y)
    split = jnp.concatenate(
        [_split_bf16(sp[:, g * group:(g + 1) * group]) for g in range(n_groups)], axis=0)
    rem = jnp.dot(split, _minus_later_keys(group, page), preferred_element_type=F32)
    run = run_ref[...]
    weights = []
    for g in range(n_groups):
        cols = slice(g * group, (g + 1) * group)
        a = jnp.exp2(y[:, cols] - sp[:, cols] + rem[g * rows:(g + 1) * rows]) * jnp.exp2(run)
        weights.append(a.astype(BF16))
        run = run - jnp.sum(sp[:, cols], axis=-1, keepdims=True)
    run_ref[...] = run
    values = jnp.concatenate([heads_side_by_side(r) for r in v_refs], axis=0)
    acc = acc_ref[...] + jnp.dot(jnp.concatenate(weights, axis=1), values, preferred_element_type=F32)
    acc_ref[...] = acc

    @pl.when(j == pl.num_programs(1) - 1)
    def _():
        kept = jnp.where(own_head, acc, 0.0)
        o_ref[...] = jnp.sum(kept.reshape(n_q, n_heads, da), axis=1)


def _sample_attn(q, k_new, v_new, bias_rows, cache_k, cache_v, page_table):
    b, n_q, da = q.shape
    n_heads = da // HEAD_DIM
    rows = n_q * n_heads
    n_pages = page_table.shape[1]
    page = cache_k.shape[1] // n_heads
    pps = PAGES_PER_STEP
    steps = n_pages // pps

    def page_spec(i):
        return pl.BlockSpec(
            (None, page * n_heads, HEAD_DIM),
            lambda bi, j, pt, i=i: (pt[bi, n_pages - 1 - (j * pps + i)], 0, 0))

    per_batch = lambda bi, j, pt: (bi, 0, 0)
    grid_spec = pltpu.PrefetchScalarGridSpec(
        num_scalar_prefetch=1,
        grid=(b, steps),
        in_specs=[
            pl.BlockSpec((None, n_q, da), per_batch),
            pl.BlockSpec((None, k_new.shape[1], da), per_batch),
            pl.BlockSpec((None, v_new.shape[1], da), per_batch),
            pl.BlockSpec((rows, page), lambda bi, j, pt: (0, 0)),
        ] + [page_spec(i) for i in range(pps)] * 2,
        out_specs=pl.BlockSpec((None, n_q, da), per_batch),
        scratch_shapes=[
            pltpu.VMEM((rows, da), F32),
            pltpu.VMEM((rows, 1), F32),
            pltpu.VMEM((rows, da), BF16),
            pltpu.VMEM((page, da), F32),
            pltpu.VMEM((page, da), F32),
        ],
    )
    return pl.pallas_call(
        functools.partial(_sample_attn_kernel, n_heads=n_heads, page=page, pages_per_step=pps),
        grid_spec=grid_spec,
        out_shape=jax.ShapeDtypeStruct((b, n_q, da), F32),
        compiler_params=_params("arbitrary", "arbitrary"),
        name="sample_attn",
    )(page_table, q, k_new, v_new, bias_rows, *([cache_k] * pps), *([cache_v] * pps))


def _row_tile(rows, target):
    return target if rows % target == 0 else rows


def _finish(x2d, o_attn, pool_in, lw, tm, blocks_per_seq=None):
    x1, h2 = _mix_out(x2d, o_attn, pool_in, lw["pool_w"], lw["pool_scale"], lw["w_out"], lw["norm2_g"], tm,
                      blocks_per_seq)
    return _ffn(x1, h2, lw["w_gate"], lw["w_up"], lw["w_down"], tm, 512)


def kernel(x_prompt, x_sample, cache_k, cache_v, state_pool, page_table, norm1_g, w_in, q_norm_g, k_norm_g,
           sb_bias, pool_w, pool_scale, w_out, norm2_g, w_gate, w_up, w_down):
    b_p, seq, dm = x_prompt.shape
    b_s, dec_seq, _ = x_sample.shape
    depth = w_in.shape[0]
    n_pool_pages, page, n_heads, _ = cache_k.shape[1:]
    da = n_heads * HEAD_DIM
    dp = state_pool.shape[-1]
    past_len = page_table.shape[1] * page

    xp = x_prompt.reshape(b_p * seq, dm)
    xs = x_sample.reshape(b_s * dec_seq, dm)
    tm_p = _row_tile(b_p * seq, 512)
    tm_s = b_s * dec_seq
    outs = [[] for _ in range(6)]
    for l in range(depth):
        lw = {
            "pool_w": pool_w[l].astype(BF16),
            "pool_scale": pool_scale[l].reshape(1, dp),
            "w_out": w_out[l].astype(BF16),
            "norm2_g": norm2_g[l].reshape(1, dm),
            "w_gate": w_gate[l].astype(BF16),
            "w_up": w_up[l].astype(BF16),
            "w_down": w_down[l].astype(BF16),
        }
        g1 = norm1_g[l].reshape(1, dm)
        w_in_l = w_in[l].astype(BF16)
        qg = q_norm_g[l].reshape(1, HEAD_DIM)
        kg = k_norm_g[l].reshape(1, HEAD_DIM)

        q, k, v, p = _in_proj(xp, g1, w_in_l, qg, kg, tm_p)
        o_attn = _prompt_attn(q.reshape(b_p, seq, da), k.reshape(b_p, seq, da), v.reshape(b_p, seq, da),
                              sb_bias[l], 256)
        outs[0].append(k.reshape(b_p, seq, n_heads, HEAD_DIM))
        outs[1].append(v.reshape(b_p, seq, n_heads, HEAD_DIM))
        outs[2].append(p.reshape(b_p, seq, dp)[:, seq - POOL_HIST:])
        xp = _finish(xp, o_attn.reshape(b_p * seq, da), p, lw, tm_p, seq // tm_p)

        q, k, v, p = _in_proj(xs, g1, w_in_l, qg, kg, tm_s)
        pad = 8 - dec_seq
        k_new = jnp.pad(k.reshape(b_s, dec_seq, da), ((0, 0), (0, pad), (0, 0)))
        v_new = jnp.pad(v.reshape(b_s, dec_seq, da), ((0, 0), (0, pad), (0, 0)))
        bias_rows = jnp.broadcast_to(jnp.tile(sb_bias[l].astype(F32), dec_seq)[:, None], (dec_seq * n_heads, page))
        o_attn = _sample_attn(q.astype(F32).reshape(b_s, dec_seq, da), k_new, v_new, bias_rows,
                              cache_k[l].reshape(n_pool_pages, page * n_heads, HEAD_DIM),
                              cache_v[l].reshape(n_pool_pages, page * n_heads, HEAD_DIM), page_table)
        p3 = p.reshape(b_s, dec_seq, dp)
        hist = jnp.pad(state_pool[l], ((0, 0), (HALO - POOL_HIST, 0), (0, 0)))
        d = _pool_delta(hist, jnp.pad(p3, ((0, 0), (0, pad), (0, 0))), past_len)
        outs[3].append(k.reshape(b_s, dec_seq, n_heads, HEAD_DIM))
        outs[4].append(v.reshape(b_s, dec_seq, n_heads, HEAD_DIM))
        outs[5].append(jnp.concatenate([state_pool[l], p3], axis=1)[:, dec_seq:])
        xs = _finish(xs, o_attn.astype(BF16).reshape(b_s * dec_seq, da), d[:, :dec_seq].reshape(b_s * dec_seq, dp),
                     lw, tm_s)

    stack = lambda xs_: jnp.stack(xs_, axis=0)
    return (xp.reshape(b_p, seq, dm), xs.reshape(b_s, dec_seq, dm), stack(outs[0]), stack(outs[1]),
            stack(outs[2]), stack(outs[3]), stack(outs[4]), stack(outs[5]))
```

```python
import functools

import jax
import jax.numpy as jnp
from jax import lax
from jax.experimental import pallas as pl
from jax.experimental.pallas import tpu as pltpu

F32 = jnp.float32
BF16 = jnp.bfloat16

HEAD_DIM = 128
POOL_WINDOWS = (2, 4, 8, 16)
POOL_HIST = max(POOL_WINDOWS) - 1
HALO = 16
EPS = 1e-6
VMEM_LIMIT = 56 * 1024 * 1024
PAGES_PER_STEP = 16
PROMPT_UNROLL = 4


def _params(*sem):
    return pltpu.CompilerParams(dimension_semantics=sem, vmem_limit_bytes=VMEM_LIMIT)


LOG2E = 1.4426950408889634
SCORE_SCALE = HEAD_DIM ** -0.5 * LOG2E
MASKED_SCORE = -1e30
SOFTPLUS2_LINEAR = 60.0


def _softplus2(y):
    return jnp.maximum(y, 0.0) + jnp.log(1.0 + jnp.exp2(-jnp.abs(y))) * LOG2E


def _split_bf16(x):
    hi = x.astype(BF16)
    lo = (x - hi.astype(F32)).astype(BF16)
    return jnp.concatenate([hi, lo], axis=1)


def _minus_later_keys(n, page=None):
    j = lax.broadcasted_iota(jnp.int32, (n, n), 0)
    s = lax.broadcasted_iota(jnp.int32, (n, n), 1)
    if page is None:
        later = j > s
    else:
        later = (j // page < s // page) | ((j // page == s // page) & (j > s))
    m = jnp.where(later, -1.0, 0.0).astype(BF16)
    return jnp.concatenate([m, m], axis=0)


def _in_proj_body(x_ref, g1_ref, weight_chunk, n_chunk, qg_ref, kg_ref, q_ref, k_ref, v_ref, p_ref, h_ref):
    j = pl.program_id(1)

    @pl.when(j == 0)
    def _():
        x = x_ref[...]
        ms = jnp.mean(x * x, axis=-1, keepdims=True)
        h_ref[...] = (x * lax.rsqrt(ms + EPS) * g1_ref[...]).astype(BF16)

    def project(c):
        return jnp.dot(h_ref[...], weight_chunk(c), preferred_element_type=F32)

    def head_norm(u, g):
        outs = []
        for i in range(u.shape[1] // HEAD_DIM):
            blk = u[:, i * HEAD_DIM:(i + 1) * HEAD_DIM]
            ms = jnp.mean(blk * blk, axis=-1, keepdims=True)
            outs.append(blk * lax.rsqrt(ms + EPS) * g)
        return jnp.concatenate(outs, axis=1)

    @pl.when(j == 0)
    def _():
        for c in range(n_chunk):
            q_ref[:, c * 256:(c + 1) * 256] = (head_norm(project(c), qg_ref[...]) * SCORE_SCALE).astype(BF16)

    @pl.when(j == 1)
    def _():
        for c in range(n_chunk):
            k_ref[:, c * 256:(c + 1) * 256] = head_norm(project(c), kg_ref[...])

    @pl.when(j == 2)
    def _():
        for c in range(n_chunk):
            v_ref[:, c * 256:(c + 1) * 256] = project(c)

    @pl.when(j == 3)
    def _():
        for c in range(n_chunk):
            p_ref[:, c * 256:(c + 1) * 256] = project(c)


def _in_proj_kernel(x_ref, g1_ref, w_ref, qg_ref, kg_ref, q_ref, k_ref, v_ref, p_ref, h_ref):
    _in_proj_body(x_ref, g1_ref, lambda c: w_ref[:, c * 256:(c + 1) * 256], w_ref.shape[1] // 256,
                  qg_ref, kg_ref, q_ref, k_ref, v_ref, p_ref, h_ref)


def _in_proj_cast_kernel(x_ref, g1_ref, w_ref, qg_ref, kg_ref, q_ref, k_ref, v_ref, p_ref, wb_ref, h_ref):
    def weight_chunk(c):
        w = w_ref[:, c * 256:(c + 1) * 256].astype(BF16)
        wb_ref[:, c * 256:(c + 1) * 256] = w
        return w

    _in_proj_body(x_ref, g1_ref, weight_chunk, w_ref.shape[1] // 256,
                  qg_ref, kg_ref, q_ref, k_ref, v_ref, p_ref, h_ref)


def _in_proj(x, g1, w_in, qg, kg, tm):
    rows, d = x.shape
    sec = w_in.shape[1] // 4
    cast = w_in.dtype == F32
    assert not cast or rows == tm, "every weight block must be visited exactly once to be cast"
    out_spec = pl.BlockSpec((tm, sec), lambda i, j: (i, 0))
    w_spec = pl.BlockSpec((d, sec), lambda i, j: (0, j))
    out_specs = [out_spec, out_spec, out_spec, out_spec]
    out_shape = [
        jax.ShapeDtypeStruct((rows, sec), BF16),
        jax.ShapeDtypeStruct((rows, sec), F32),
        jax.ShapeDtypeStruct((rows, sec), F32),
        jax.ShapeDtypeStruct((rows, sec), F32),
    ]
    if cast:
        out_specs.append(w_spec)
        out_shape.append(jax.ShapeDtypeStruct(w_in.shape, BF16))
    return pl.pallas_call(
        _in_proj_cast_kernel if cast else _in_proj_kernel,
        grid=(rows // tm, 4),
        in_specs=[
            pl.BlockSpec((tm, d), lambda i, j: (i, 0)),
            pl.BlockSpec((1, d), lambda i, j: (0, 0)),
            w_spec,
            pl.BlockSpec((1, HEAD_DIM), lambda i, j: (0, 0)),
            pl.BlockSpec((1, HEAD_DIM), lambda i, j: (0, 0)),
        ],
        out_specs=out_specs,
        out_shape=out_shape,
        scratch_shapes=[pltpu.VMEM((tm, d), BF16)],
        compiler_params=_params("arbitrary", "arbitrary"),
        name="in_proj",
    )(x, g1, w_in, qg, kg)


def _prompt_block_tables(n_q_blocks, blk):
    blocks = [(qi, qi - n) for qi in range(n_q_blocks) for n in range(qi + 1)]
    pad = -(len(blocks) + 2) % PROMPT_UNROLL
    n_iter = len(blocks) + 2 + pad
    rows = [[0] * n_iter for _ in range(6)]
    for e in range(n_iter):
        qi, kj = blocks[min(max(e - pad, 0), len(blocks) - 1)]
        rows[0][e], rows[1][e], rows[2][e] = qi * blk, kj * blk, int(qi == kj)
        if e - pad - 2 >= 0:
            qi, kj = blocks[e - pad - 2]
            rows[3][e], rows[4][e], rows[5][e] = kj * blk, qi * blk, int(qi == kj)
    return jnp.asarray(rows, jnp.int32), n_iter


def _prompt_attn_kernel(tbl_ref, bias_ref, q_ref, k_ref, v_ref, o_ref, kb_ref, vb_ref, later_ref, bm_ref,
                        y_ref, w_ref, spent_ref, run_ref, acc_ref, *, blk, n_iter):
    h = pl.program_id(1)
    kb_ref[...] = k_ref[...].astype(BF16)
    vb_ref[...] = v_ref[...].astype(BF16)
    y_ref[...] = jnp.zeros(y_ref.shape, F32)
    w_ref[...] = jnp.zeros(w_ref.shape, F32)
    spent_ref[...] = jnp.zeros(spent_ref.shape, F32)
    run_ref[...] = jnp.zeros(run_ref.shape, F32)
    acc_ref[...] = jnp.zeros(acc_ref.shape, F32)
    later_ref[...] = _minus_later_keys(blk)[:blk]
    bias = jnp.full((blk, blk), bias_ref[h] * LOG2E, F32)
    visible = (lax.broadcasted_iota(jnp.int32, (blk, blk), 1) < lax.broadcasted_iota(jnp.int32, (blk, blk), 0))
    bm_ref[0] = bias
    bm_ref[1] = jnp.where(visible, bias, MASKED_SCORE)

    def iteration(e, par):
        q = q_ref[pl.ds(pl.multiple_of(tbl_ref[0, e], blk), blk), :]
        kb = kb_ref[pl.ds(pl.multiple_of(tbl_ref[1, e], blk), blk), :]
        y = lax.dot_general(q, kb, (((1,), (1,)), ((), ())), preferred_element_type=F32)
        y_ref[par] = y + bm_ref[tbl_ref[2, e]]

        y = y_ref[1 - par]
        sp = jnp.maximum(jnp.log(1.0 + jnp.exp2(jnp.minimum(y, SOFTPLUS2_LINEAR))) * LOG2E, y)
        spent_ref[1 - par] = jnp.sum(sp, axis=-1, keepdims=True)
        rem = jnp.dot(sp.astype(BF16), later_ref[...], preferred_element_type=F32)
        w_ref[1 - par] = y - sp + rem

        first = tbl_ref[5, e] == 1
        a = jnp.exp2(w_ref[par]).astype(BF16)
        vb = vb_ref[pl.ds(pl.multiple_of(tbl_ref[3, e], blk), blk), :]
        pv = jnp.dot(a, vb, preferred_element_type=F32)
        run = jnp.where(first, 0.0, run_ref[...])
        acc = jnp.where(first, 0.0, acc_ref[...]) + pv * jnp.exp2(run)
        run_ref[...] = run - spent_ref[par]
        acc_ref[...] = acc
        o_ref[pl.ds(pl.multiple_of(tbl_ref[4, e], blk), blk), :] = acc.astype(BF16)

    def unrolled(i, _):
        for u in range(PROMPT_UNROLL):
            iteration(PROMPT_UNROLL * i + u, u % 2)
        return 0

    lax.fori_loop(0, n_iter // PROMPT_UNROLL, unrolled, 0)


def _prompt_attn(q, k, v, sb_bias, blk):
    b, t, da = q.shape
    tables, n_iter = _prompt_block_tables(t // blk, blk)
    spec = pl.BlockSpec((None, t, HEAD_DIM), lambda bi, h, tbl: (bi, 0, h))
    grid_spec = pltpu.PrefetchScalarGridSpec(
        num_scalar_prefetch=1,
        grid=(b, da // HEAD_DIM),
        in_specs=[pl.BlockSpec(memory_space=pltpu.SMEM), spec, spec, spec],
        out_specs=spec,
        scratch_shapes=[
            pltpu.VMEM((t, HEAD_DIM), BF16),
            pltpu.VMEM((t, HEAD_DIM), BF16),
            pltpu.VMEM((blk, blk), BF16),
            pltpu.VMEM((2, blk, blk), F32),
            pltpu.VMEM((2, blk, blk), F32),
            pltpu.VMEM((2, blk, blk), F32),
            pltpu.VMEM((2, blk, 1), F32),
            pltpu.VMEM((blk, 1), F32),
            pltpu.VMEM((blk, HEAD_DIM), F32),
        ],
    )
    return pl.pallas_call(
        functools.partial(_prompt_attn_kernel, blk=blk, n_iter=n_iter),
        grid_spec=grid_spec,
        out_shape=jax.ShapeDtypeStruct((b, t, da), BF16),
        compiler_params=_params("arbitrary", "arbitrary"),
        name="prompt_attn",
    )(tables, sb_bias, q, k, v)


def _window_deltas(ext_ref, tm, pos):
    group = ext_ref.shape[1] // len(POOL_WINDOWS)
    deltas = []
    for g, w in enumerate(POOL_WINDOWS):
        rows = ext_ref[:, pl.ds(g * group, group)]
        total, shift = rows, 1
        while shift < w:
            total = total + pltpu.roll(total, shift, axis=0)
            shift *= 2
        cnt = jnp.minimum(pos + 1, w).astype(F32)
        deltas.append((total[HALO:] / cnt - rows[HALO:]).astype(BF16))
    return deltas


def _pool_delta_kernel(halo_ref, p_ref, d_ref, ext_ref, *, base_pos):
    n_seq, tm, width = p_ref.shape
    group = width // len(POOL_WINDOWS)
    pos = base_pos + lax.broadcasted_iota(jnp.int32, (tm, 1), 0)

    def one_sequence(s, _):
        ext_ref[pl.ds(0, HALO), :] = halo_ref[s]
        ext_ref[pl.ds(HALO, tm), :] = p_ref[s]
        for g, delta in enumerate(_window_deltas(ext_ref, tm, pos)):
            d_ref[s, :, pl.ds(g * group, group)] = delta
        return 0

    lax.fori_loop(0, n_seq, one_sequence, 0)


def _pool_delta(history, p, base_pos):
    n_seq, tm, c = p.shape
    whole = lambda i: (0, 0, 0)
    return pl.pallas_call(
        functools.partial(_pool_delta_kernel, base_pos=base_pos),
        grid=(1,),
        in_specs=[pl.BlockSpec((n_seq, HALO, c), whole), pl.BlockSpec((n_seq, tm, c), whole)],
        out_specs=pl.BlockSpec((n_seq, tm, c), whole),
        out_shape=jax.ShapeDtypeStruct((n_seq, tm, c), BF16),
        scratch_shapes=[pltpu.VMEM((HALO + tm, c), F32)],
        compiler_params=_params("arbitrary"),
        name="pool_delta",
    )(history, p)


def _mix_finish(x_ref, oa_ref, deltas, pw_ref, ps_ref, wo_ref, g2_ref, x1_ref, h2_ref):
    da = oa_ref.shape[1]
    group = pw_ref.shape[1]
    y = jnp.dot(oa_ref[...], wo_ref[:da, :], preferred_element_type=F32)
    pooled = []
    for g, delta in enumerate(deltas):
        y_g = jnp.dot(delta, pw_ref[g], preferred_element_type=F32)
        pooled.append((y_g * ps_ref[:, g * group:(g + 1) * group]).astype(BF16))
    y = y + jnp.dot(jnp.concatenate(pooled, axis=1), wo_ref[da:, :], preferred_element_type=F32)
    x1 = x_ref[...] + y
    x1_ref[...] = x1
    ms = jnp.mean(x1 * x1, axis=-1, keepdims=True)
    h2_ref[...] = (x1 * lax.rsqrt(ms + EPS) * g2_ref[...]).astype(BF16)


def _mix_out_kernel(x_ref, oa_ref, d_ref, pw_ref, ps_ref, wo_ref, g2_ref, x1_ref, h2_ref):
    group = pw_ref.shape[1]
    deltas = [d_ref[:, g * group:(g + 1) * group] for g in range(pw_ref.shape[0])]
    _mix_finish(x_ref, oa_ref, deltas, pw_ref, ps_ref, wo_ref, g2_ref, x1_ref, h2_ref)


def _mix_out_pool_kernel(x_ref, oa_ref, halo_ref, p_ref, pw_ref, ps_ref, wo_ref, g2_ref, x1_ref, h2_ref,
                         ext_ref, *, blocks_per_seq):
    tm = p_ref.shape[0]
    seq_blk = pl.program_id(0) % blocks_per_seq
    ext_ref[pl.ds(0, HALO), :] = jnp.where(seq_blk == 0, 0.0, halo_ref[...])
    ext_ref[pl.ds(HALO, tm), :] = p_ref[...]
    pos = seq_blk * tm + lax.broadcasted_iota(jnp.int32, (tm, 1), 0)
    deltas = _window_deltas(ext_ref, tm, pos)
    _mix_finish(x_ref, oa_ref, deltas, pw_ref, ps_ref, wo_ref, g2_ref, x1_ref, h2_ref)


def _mix_out(x, oa, pool_in, pool_w, pool_scale, w_out, g2, tm, blocks_per_seq=None):
    rows, dm = x.shape
    da = oa.shape[1]
    dp = pool_in.shape[1]
    row = lambda i: (i, 0)
    fixed2 = lambda i: (0, 0)
    weight_specs = [
        pl.BlockSpec(pool_w.shape, lambda i: (0, 0, 0)),
        pl.BlockSpec((1, dp), fixed2),
        pl.BlockSpec(w_out.shape, fixed2),
        pl.BlockSpec((1, dm), fixed2),
    ]
    common = dict(
        grid=(rows // tm,),
        out_specs=[pl.BlockSpec((tm, dm), row), pl.BlockSpec((tm, dm), row)],
        out_shape=[jax.ShapeDtypeStruct((rows, dm), F32), jax.ShapeDtypeStruct((rows, dm), BF16)],
        compiler_params=_params("arbitrary"),
        name="mix_out",
    )
    if blocks_per_seq is None:
        return pl.pallas_call(
            _mix_out_kernel,
            in_specs=[pl.BlockSpec((tm, dm), row), pl.BlockSpec((tm, da), row), pl.BlockSpec((tm, dp), row)]
            + weight_specs,
            **common,
        )(x, oa, pool_in, pool_w, pool_scale, w_out, g2)
    halo_per_block = tm // HALO
    return pl.pallas_call(
        functools.partial(_mix_out_pool_kernel, blocks_per_seq=blocks_per_seq),
        in_specs=[
            pl.BlockSpec((tm, dm), row),
            pl.BlockSpec((tm, da), row),
            pl.BlockSpec((None, HALO, dp), lambda i: (jnp.maximum(i * halo_per_block - 1, 0), 0, 0)),
            pl.BlockSpec((tm, dp), row),
        ] + weight_specs,
        scratch_shapes=[pltpu.VMEM((HALO + tm, dp), F32)],
        **common,
    )(x, oa, pool_in.reshape(rows // HALO, HALO, dp), pool_in, pool_w, pool_scale, w_out, g2)


def _ffn_body(x1_ref, h2_ref, weight, o_ref):
    @pl.when(pl.program_id(1) == 0)
    def _():
        o_ref[...] = x1_ref[...]

    h2 = h2_ref[...]
    g = jnp.dot(h2, weight("gate"), preferred_element_type=F32)
    u = jnp.dot(h2, weight("up"), preferred_element_type=F32)
    ff = (g / (1.0 + jnp.exp(-g))) * u
    o_ref[...] += jnp.dot(ff.astype(BF16), weight("down"), preferred_element_type=F32)


def _ffn_kernel(x1_ref, h2_ref, wg_ref, wu_ref, wd_ref, o_ref):
    refs = {"gate": wg_ref, "up": wu_ref, "down": wd_ref}
    _ffn_body(x1_ref, h2_ref, lambda name: refs[name][...], o_ref)


def _ffn_cast_kernel(x1_ref, h2_ref, wg_ref, wu_ref, wd_ref, o_ref, wgb_ref, wub_ref, wdb_ref):
    refs = {"gate": (wg_ref, wgb_ref), "up": (wu_ref, wub_ref), "down": (wd_ref, wdb_ref)}

    def weight(name):
        src, dst = refs[name]
        w = src[...].astype(BF16)
        dst[...] = w
        return w

    _ffn_body(x1_ref, h2_ref, weight, o_ref)


def _ffn(x1, h2, wg, wu, wd, tm, tf):
    rows, dm = x1.shape
    dff = wg.shape[1]
    cast = wg.dtype == F32
    assert not cast or rows == tm, "every weight block must be visited exactly once to be cast"
    weight_specs = [
        pl.BlockSpec((dm, tf), lambda i, f: (0, f)),
        pl.BlockSpec((dm, tf), lambda i, f: (0, f)),
        pl.BlockSpec((tf, dm), lambda i, f: (f, 0)),
    ]
    out_specs = [pl.BlockSpec((tm, dm), lambda i, f: (i, 0))]
    out_shape = [jax.ShapeDtypeStruct((rows, dm), F32)]
    if cast:
        out_specs += weight_specs
        out_shape += [jax.ShapeDtypeStruct(w.shape, BF16) for w in (wg, wu, wd)]
    out = pl.pallas_call(
        _ffn_cast_kernel if cast else _ffn_kernel,
        grid=(rows // tm, dff // tf),
        in_specs=[pl.BlockSpec((tm, dm), lambda i, f: (i, 0)), pl.BlockSpec((tm, dm), lambda i, f: (i, 0))]
        + weight_specs,
        out_specs=out_specs,
        out_shape=out_shape,
        compiler_params=_params("arbitrary", "arbitrary"),
        name="ffn",
    )(x1, h2, wg, wu, wd)
    return out if cast else out[0]


def _sample_attn_kernel(pt_ref, q_ref, kn_ref, vn_ref, bias_ref, *refs, n_heads, page, pages_per_step):
    del pt_ref
    k_refs = refs[:pages_per_step]
    v_refs = refs[pages_per_step:2 * pages_per_step]
    o_ref, acc_ref, run_ref, qbd_ref, kpad_ref, vpad_ref = refs[2 * pages_per_step:]
    j = pl.program_id(1)
    n_q = q_ref.shape[0]
    rows = n_q * n_heads
    da = n_heads * HEAD_DIM
    group = 2 * page
    n_groups = pages_per_step // 2
    head_of_row = lax.broadcasted_iota(jnp.int32, (rows, da), 0) % n_heads
    head_of_col = lax.broadcasted_iota(jnp.int32, (rows, da), 1) // HEAD_DIM
    own_head = head_of_row == head_of_col
    bias = bias_ref[...] * LOG2E

    def scores(keys, n_tiles):
        y = lax.dot_general(qbd_ref[...], keys, (((1,), (1,)), ((), ())), preferred_element_type=F32)
        return y + jnp.concatenate([bias] * n_tiles, axis=1)

    def heads_side_by_side(ref):
        return jnp.concatenate(
            [ref[pl.ds(h, page, stride=n_heads), :] for h in range(n_heads)], axis=1).astype(BF16)

    @pl.when(j == 0)
    def _():
        q = q_ref[...]
        q_rows = jnp.concatenate(
            [jnp.broadcast_to(q[t:t + 1, :], (n_heads, da)) for t in range(n_q)], axis=0)
        qbd_ref[...] = jnp.where(own_head, q_rows, 0.0).astype(BF16)
        n_new = kn_ref.shape[0]
        kpad_ref[...] = jnp.zeros(kpad_ref.shape, F32)
        vpad_ref[...] = jnp.zeros(vpad_ref.shape, F32)
        kpad_ref[pl.ds(0, n_new), :] = kn_ref[...]
        vpad_ref[pl.ds(0, n_new), :] = vn_ref[...]
        t_of_row = lax.broadcasted_iota(jnp.int32, (rows, page), 0) // n_heads
        key = lax.broadcasted_iota(jnp.int32, (rows, page), 1)
        causal = key < t_of_row
        y = scores(kpad_ref[...].astype(BF16), 1)
        sp = jnp.where(causal, _softplus2(y), 0.0)
        rem = jnp.dot(_split_bf16(sp), _minus_later_keys(page), preferred_element_type=F32)
        a = jnp.where(causal, jnp.exp2(y - sp + rem), 0.0)
        acc_ref[...] = jnp.dot(a.astype(BF16), vpad_ref[...].astype(BF16), preferred_element_type=F32)
        run_ref[...] = -jnp.sum(sp, axis=-1, keepdims=True)

    keys = jnp.concatenate([heads_side_by_side(r) for r in k_refs], axis=0)
    y = scores(keys, pages_per_step)
    sp = _softplus2(y)
    split = jnp.concatenate(
        [_split_bf16(sp[:, g * group:(g + 1) * group]) for g in range(n_groups)], axis=0)
    rem = jnp.dot(split, _minus_later_keys(group, page), preferred_element_type=F32)
    run = run_ref[...]
    weights = []
    for g in range(n_groups):
        cols = slice(g * group, (g + 1) * group)
        a = jnp.exp2(y[:, cols] - sp[:, cols] + rem[g * rows:(g + 1) * rows]) * jnp.exp2(run)
        weights.append(a.astype(BF16))
        run = run - jnp.sum(sp[:, cols], axis=-1, keepdims=True)
    run_ref[...] = run
    values = jnp.concatenate([heads_side_by_side(r) for r in v_refs], axis=0)
    acc = acc_ref[...] + jnp.dot(jnp.concatenate(weights, axis=1), values, preferred_element_type=F32)
    acc_ref[...] = acc

    @pl.when(j == pl.num_programs(1) - 1)
    def _():
        kept = jnp.where(own_head, acc, 0.0)
        o_ref[...] = jnp.sum(kept.reshape(n_q, n_heads, da), axis=1)


def _sample_attn(q, k_new, v_new, bias_rows, cache_k, cache_v, page_table):
    b, n_q, da = q.shape
    n_heads = da // HEAD_DIM
    rows = n_q * n_heads
    n_pages = page_table.shape[1]
    page = cache_k.shape[1] // n_heads
    pps = PAGES_PER_STEP
    steps = n_pages // pps

    def page_spec(i):
        return pl.BlockSpec(
            (None, page * n_heads, HEAD_DIM),
            lambda bi, j, pt, i=i: (pt[bi, n_pages - 1 - (j * pps + i)], 0, 0))

    per_batch = lambda bi, j, pt: (bi, 0, 0)
    grid_spec = pltpu.PrefetchScalarGridSpec(
        num_scalar_prefetch=1,
        grid=(b, steps),
        in_specs=[
            pl.BlockSpec((None, n_q, da), per_batch),
            pl.BlockSpec((None, k_new.shape[1], da), per_batch),
            pl.BlockSpec((None, v_new.shape[1], da), per_batch),
            pl.BlockSpec((rows, page), lambda bi, j, pt: (0, 0)),
        ] + [page_spec(i) for i in range(pps)] * 2,
        out_specs=pl.BlockSpec((None, n_q, da), per_batch),
        scratch_shapes=[
            pltpu.VMEM((rows, da), F32),
            pltpu.VMEM((rows, 1), F32),
            pltpu.VMEM((rows, da), BF16),
            pltpu.VMEM((page, da), F32),
            pltpu.VMEM((page, da), F32),
        ],
    )
    return pl.pallas_call(
        functools.partial(_sample_attn_kernel, n_heads=n_heads, page=page, pages_per_step=pps),
        grid_spec=grid_spec,
        out_shape=jax.ShapeDtypeStruct((b, n_q, da), F32),
        compiler_params=_params("arbitrary", "arbitrary"),
        name="sample_attn",
    )(page_table, q, k_new, v_new, bias_rows, *([cache_k] * pps), *([cache_v] * pps))


def _row_tile(rows, target):
    return target if rows % target == 0 else rows


def kernel(x_prompt, x_sample, cache_k, cache_v, state_pool, page_table, norm1_g, w_in, q_norm_g, k_norm_g,
           sb_bias, pool_w, pool_scale, w_out, norm2_g, w_gate, w_up, w_down):
    b_p, seq, dm = x_prompt.shape
    b_s, dec_seq, _ = x_sample.shape
    depth = w_in.shape[0]
    n_pool_pages, page, n_heads, _ = cache_k.shape[1:]
    da = n_heads * HEAD_DIM
    dp = state_pool.shape[-1]
    past_len = page_table.shape[1] * page

    xp = x_prompt.reshape(b_p * seq, dm)
    xs = x_sample.reshape(b_s * dec_seq, dm)
    tm_p = _row_tile(b_p * seq, 512)
    tm_s = b_s * dec_seq
    outs = [[] for _ in range(6)]
    for l in range(depth):
        g1 = norm1_g[l].reshape(1, dm)
        g2 = norm2_g[l].reshape(1, dm)
        qg = q_norm_g[l].reshape(1, HEAD_DIM)
        kg = k_norm_g[l].reshape(1, HEAD_DIM)
        pool_w_l = pool_w[l].astype(BF16)
        pool_scale_l = pool_scale[l].reshape(1, dp)
        w_out_l = w_out[l].astype(BF16)

        q, k, v, p, w_in_l = _in_proj(xs, g1, w_in[l], qg, kg, tm_s)
        pad = 8 - dec_seq
        k_new = jnp.pad(k.reshape(b_s, dec_seq, da), ((0, 0), (0, pad), (0, 0)))
        v_new = jnp.pad(v.reshape(b_s, dec_seq, da), ((0, 0), (0, pad), (0, 0)))
        bias_rows = jnp.broadcast_to(jnp.tile(sb_bias[l].astype(F32), dec_seq)[:, None], (dec_seq * n_heads, page))
        o_attn = _sample_attn(q.astype(F32).reshape(b_s, dec_seq, da), k_new, v_new, bias_rows,
                              cache_k[l].reshape(n_pool_pages, page * n_heads, HEAD_DIM),
                              cache_v[l].reshape(n_pool_pages, page * n_heads, HEAD_DIM), page_table)
        p3 = p.reshape(b_s, dec_seq, dp)
        hist = jnp.pad(state_pool[l], ((0, 0), (HALO - POOL_HIST, 0), (0, 0)))
        d = _pool_delta(hist, jnp.pad(p3, ((0, 0), (0, pad), (0, 0))), past_len)
        outs[3].append(k.reshape(b_s, dec_seq, n_heads, HEAD_DIM))
        outs[4].append(v.reshape(b_s, dec_seq, n_heads, HEAD_DIM))
        outs[5].append(jnp.concatenate([state_pool[l], p3], axis=1)[:, dec_seq:])
        x1, h2 = _mix_out(xs, o_attn.astype(BF16).reshape(b_s * dec_seq, da),
                          d[:, :dec_seq].reshape(b_s * dec_seq, dp), pool_w_l, pool_scale_l, w_out_l, g2, tm_s)
        xs, w_gate_l, w_up_l, w_down_l = _ffn(x1, h2, w_gate[l], w_up[l], w_down[l], tm_s, 512)

        q, k, v, p = _in_proj(xp, g1, w_in_l, qg, kg, tm_p)
        o_attn = _prompt_attn(q.reshape(b_p, seq, da), k.reshape(b_p, seq, da), v.reshape(b_p, seq, da),
                              sb_bias[l], 256)
        outs[0].append(k.reshape(b_p, seq, n_heads, HEAD_DIM))
        outs[1].append(v.reshape(b_p, seq, n_heads, HEAD_DIM))
        outs[2].append(p.reshape(b_p, seq, dp)[:, seq - POOL_HIST:])
        x1, h2 = _mix_out(xp, o_attn.reshape(b_p * seq, da), p, pool_w_l, pool_scale_l, w_out_l, g2, tm_p,
                          seq // tm_p)
        xp = _ffn(x1, h2, w_gate_l, w_up_l, w_down_l, tm_p, 512)

    stack = lambda xs_: jnp.stack(xs_, axis=0)
    return (xp.reshape(b_p, seq, dm), xs.reshape(b_s, dec_seq, dm), stack(outs[0]), stack(outs[1]),
            stack(outs[2]), stack(outs[3]), stack(outs[4]), stack(outs[5]))
```

```python
import functools

import jax
import jax.numpy as jnp
from jax import lax
from jax.experimental import pallas as pl
from jax.experimental.pallas import tpu as pltpu

F32 = jnp.float32
BF16 = jnp.bfloat16

HEAD_DIM = 128
POOL_WINDOWS = (2, 4, 8, 16)
POOL_HIST = max(POOL_WINDOWS) - 1
HALO = 16
EPS = 1e-6
VMEM_LIMIT = 56 * 1024 * 1024
PAGES_PER_STEP = 16
PROMPT_UNROLL = 2


def _params(*sem):
    return pltpu.CompilerParams(dimension_semantics=sem, vmem_limit_bytes=VMEM_LIMIT)


LOG2E = 1.4426950408889634
SCORE_SCALE = HEAD_DIM ** -0.5 * LOG2E
MASKED_SCORE = -1e30
SOFTPLUS2_LINEAR = 60.0


def _softplus2(y):
    return jnp.maximum(jnp.log(1.0 + jnp.exp2(jnp.minimum(y, SOFTPLUS2_LINEAR))) * LOG2E, y)


def _split_bf16(x):
    hi = x.astype(BF16)
    lo = (x - hi.astype(F32)).astype(BF16)
    return jnp.concatenate([hi, lo], axis=1)


def _minus_later_keys(n, page=None):
    j = lax.broadcasted_iota(jnp.int32, (n, n), 0)
    s = lax.broadcasted_iota(jnp.int32, (n, n), 1)
    if page is None:
        later = j > s
    else:
        later = (j // page < s // page) | ((j // page == s // page) & (j > s))
    m = jnp.where(later, -1.0, 0.0).astype(BF16)
    return jnp.concatenate([m, m], axis=0)


def _in_proj_body(x_ref, g1_ref, weight_chunk, n_chunk, qg_ref, kg_ref, q_ref, k_ref, v_ref, p_ref, h_ref):
    j = pl.program_id(1)

    @pl.when(j == 0)
    def _():
        x = x_ref[...]
        ms = jnp.mean(x * x, axis=-1, keepdims=True)
        h_ref[...] = (x * lax.rsqrt(ms + EPS) * g1_ref[...]).astype(BF16)

    def project(c):
        return jnp.dot(h_ref[...], weight_chunk(c), preferred_element_type=F32)

    def head_norm(u, g):
        outs = []
        for i in range(u.shape[1] // HEAD_DIM):
            blk = u[:, i * HEAD_DIM:(i + 1) * HEAD_DIM]
            ms = jnp.mean(blk * blk, axis=-1, keepdims=True)
            outs.append(blk * lax.rsqrt(ms + EPS) * g)
        return jnp.concatenate(outs, axis=1)

    @pl.when(j == 0)
    def _():
        for c in range(n_chunk):
            q_ref[:, c * 256:(c + 1) * 256] = (head_norm(project(c), qg_ref[...]) * SCORE_SCALE).astype(BF16)

    @pl.when(j == 1)
    def _():
        for c in range(n_chunk):
            k_ref[:, c * 256:(c + 1) * 256] = head_norm(project(c), kg_ref[...])

    @pl.when(j == 2)
    def _():
        for c in range(n_chunk):
            v_ref[:, c * 256:(c + 1) * 256] = project(c)

    @pl.when(j == 3)
    def _():
        for c in range(n_chunk):
            p_ref[:, c * 256:(c + 1) * 256] = project(c)


def _in_proj_kernel(x_ref, g1_ref, w_ref, qg_ref, kg_ref, q_ref, k_ref, v_ref, p_ref, h_ref):
    _in_proj_body(x_ref, g1_ref, lambda c: w_ref[:, c * 256:(c + 1) * 256], w_ref.shape[1] // 256,
                  qg_ref, kg_ref, q_ref, k_ref, v_ref, p_ref, h_ref)


def _in_proj_cast_kernel(x_ref, g1_ref, w_ref, qg_ref, kg_ref, q_ref, k_ref, v_ref, p_ref, wb_ref, h_ref):
    def weight_chunk(c):
        w = w_ref[:, c * 256:(c + 1) * 256].astype(BF16)
        wb_ref[:, c * 256:(c + 1) * 256] = w
        return w

    _in_proj_body(x_ref, g1_ref, weight_chunk, w_ref.shape[1] // 256,
                  qg_ref, kg_ref, q_ref, k_ref, v_ref, p_ref, h_ref)


def _in_proj(x, g1, w_in, qg, kg, tm):
    rows, d = x.shape
    sec = w_in.shape[1] // 4
    cast = w_in.dtype == F32
    assert not cast or rows == tm, "every weight block must be visited exactly once to be cast"
    out_spec = pl.BlockSpec((tm, sec), lambda i, j: (i, 0))
    w_spec = pl.BlockSpec((d, sec), lambda i, j: (0, j))
    out_specs = [out_spec, out_spec, out_spec, out_spec]
    out_shape = [
        jax.ShapeDtypeStruct((rows, sec), BF16),
        jax.ShapeDtypeStruct((rows, sec), F32),
        jax.ShapeDtypeStruct((rows, sec), F32),
        jax.ShapeDtypeStruct((rows, sec), F32),
    ]
    if cast:
        out_specs.append(w_spec)
        out_shape.append(jax.ShapeDtypeStruct(w_in.shape, BF16))
    return pl.pallas_call(
        _in_proj_cast_kernel if cast else _in_proj_kernel,
        grid=(rows // tm, 4),
        in_specs=[
            pl.BlockSpec((tm, d), lambda i, j: (i, 0)),
            pl.BlockSpec((1, d), lambda i, j: (0, 0)),
            w_spec,
            pl.BlockSpec((1, HEAD_DIM), lambda i, j: (0, 0)),
            pl.BlockSpec((1, HEAD_DIM), lambda i, j: (0, 0)),
        ],
        out_specs=out_specs,
        out_shape=out_shape,
        scratch_shapes=[pltpu.VMEM((tm, d), BF16)],
        compiler_params=_params("arbitrary", "arbitrary"),
        name="in_proj",
    )(x, g1, w_in, qg, kg)


def _prompt_block_tables(n_q_blocks, blk):
    blocks = [(qi, qi - n) for qi in range(n_q_blocks) for n in range(qi + 1)]
    pad = -(len(blocks) + 2) % PROMPT_UNROLL
    n_iter = len(blocks) + 2 + pad
    rows = [[0] * n_iter for _ in range(6)]
    for e in range(n_iter):
        qi, kj = blocks[min(max(e - pad, 0), len(blocks) - 1)]
        rows[0][e], rows[1][e], rows[2][e] = qi * blk, kj * blk, int(qi == kj)
        if e - pad - 2 >= 0:
            qi, kj = blocks[e - pad - 2]
            rows[3][e], rows[4][e], rows[5][e] = kj * blk, qi * blk, int(qi == kj)
    return jnp.asarray(rows, jnp.int32), n_iter


def _prompt_attn_kernel(tbl_ref, bias_ref, q_ref, k_ref, v_ref, o_ref, kb_ref, vb_ref, later_ref, bm_ref,
                        y_ref, w_ref, spent_ref, run_ref, acc_ref, *, blk, n_iter):
    h = pl.program_id(1)
    kb_ref[...] = k_ref[...].astype(BF16)
    vb_ref[...] = v_ref[...].astype(BF16)
    y_ref[...] = jnp.zeros(y_ref.shape, F32)
    w_ref[...] = jnp.zeros(w_ref.shape, F32)
    spent_ref[...] = jnp.zeros(spent_ref.shape, F32)
    run_ref[...] = jnp.zeros(run_ref.shape, F32)
    acc_ref[...] = jnp.zeros(acc_ref.shape, F32)
    later_ref[...] = _minus_later_keys(blk)[:blk]
    bias = jnp.full((blk, blk), bias_ref[h] * LOG2E, F32)
    visible = (lax.broadcasted_iota(jnp.int32, (blk, blk), 1) < lax.broadcasted_iota(jnp.int32, (blk, blk), 0))
    bm_ref[0] = bias
    bm_ref[1] = jnp.where(visible, bias, MASKED_SCORE)

    def iteration(e, par):
        q = q_ref[pl.ds(pl.multiple_of(tbl_ref[0, e], blk), blk), :]
        kb = kb_ref[pl.ds(pl.multiple_of(tbl_ref[1, e], blk), blk), :]
        y = lax.dot_general(q, kb, (((1,), (1,)), ((), ())), preferred_element_type=F32)
        y_ref[par] = y + bm_ref[tbl_ref[2, e]]

        y = y_ref[1 - par]
        sp = _softplus2(y)
        spent_ref[1 - par] = jnp.sum(sp, axis=-1, keepdims=True)
        rem = jnp.dot(sp.astype(BF16), later_ref[...], preferred_element_type=F32)
        w_ref[1 - par] = y - sp + rem

        first = tbl_ref[5, e] == 1
        a = jnp.exp2(w_ref[par]).astype(BF16)
        vb = vb_ref[pl.ds(pl.multiple_of(tbl_ref[3, e], blk), blk), :]
        pv = jnp.dot(a, vb, preferred_element_type=F32)
        run = jnp.where(first, 0.0, run_ref[...])
        acc = jnp.where(first, 0.0, acc_ref[...]) + pv * jnp.exp2(run)
        run_ref[...] = run - spent_ref[par]
        acc_ref[...] = acc
        o_ref[pl.ds(pl.multiple_of(tbl_ref[4, e], blk), blk), :] = acc.astype(BF16)

    def unrolled(i, _):
        for u in range(PROMPT_UNROLL):
            iteration(PROMPT_UNROLL * i + u, u % 2)
        return 0

    lax.fori_loop(0, n_iter // PROMPT_UNROLL, unrolled, 0)


def _prompt_attn(q, k, v, sb_bias, blk):
    b, t, da = q.shape
    tables, n_iter = _prompt_block_tables(t // blk, blk)
    spec = pl.BlockSpec((None, t, HEAD_DIM), lambda bi, h, tbl: (bi, 0, h))
    grid_spec = pltpu.PrefetchScalarGridSpec(
        num_scalar_prefetch=1,
        grid=(b, da // HEAD_DIM),
        in_specs=[pl.BlockSpec(memory_space=pltpu.SMEM), spec, spec, spec],
        out_specs=spec,
        scratch_shapes=[
            pltpu.VMEM((t, HEAD_DIM), BF16),
            pltpu.VMEM((t, HEAD_DIM), BF16),
            pltpu.VMEM((blk, blk), BF16),
            pltpu.VMEM((2, blk, blk), F32),
            pltpu.VMEM((2, blk, blk), F32),
            pltpu.VMEM((2, blk, blk), F32),
            pltpu.VMEM((2, blk, 1), F32),
            pltpu.VMEM((blk, 1), F32),
            pltpu.VMEM((blk, HEAD_DIM), F32),
        ],
    )
    return pl.pallas_call(
        functools.partial(_prompt_attn_kernel, blk=blk, n_iter=n_iter),
        grid_spec=grid_spec,
        out_shape=jax.ShapeDtypeStruct((b, t, da), BF16),
        compiler_params=_params("arbitrary", "arbitrary"),
        name="prompt_attn",
    )(tables, sb_bias, q, k, v)


def _window_deltas(ext_ref, tm, pos):
    group = ext_ref.shape[1] // len(POOL_WINDOWS)
    deltas = []
    for g, w in enumerate(POOL_WINDOWS):
        rows = ext_ref[:, pl.ds(g * group, group)]
        total, shift = rows, 1
        while shift < w:
            total = total + pltpu.roll(total, shift, axis=0)
            shift *= 2
        cnt = jnp.minimum(pos + 1, w).astype(F32)
        deltas.append((total[HALO:] / cnt - rows[HALO:]).astype(BF16))
    return deltas


def _pool_delta_kernel(halo_ref, p_ref, d_ref, ext_ref, *, base_pos):
    n_seq, tm, width = p_ref.shape
    group = width // len(POOL_WINDOWS)
    pos = base_pos + lax.broadcasted_iota(jnp.int32, (tm, 1), 0)

    def one_sequence(s, _):
        ext_ref[pl.ds(0, HALO), :] = halo_ref[s]
        ext_ref[pl.ds(HALO, tm), :] = p_ref[s]
        for g, delta in enumerate(_window_deltas(ext_ref, tm, pos)):
            d_ref[s, :, pl.ds(g * group, group)] = delta
        return 0

    lax.fori_loop(0, n_seq, one_sequence, 0)


def _pool_delta(history, p, base_pos):
    n_seq, tm, c = p.shape
    whole = lambda i: (0, 0, 0)
    return pl.pallas_call(
        functools.partial(_pool_delta_kernel, base_pos=base_pos),
        grid=(1,),
        in_specs=[pl.BlockSpec((n_seq, HALO, c), whole), pl.BlockSpec((n_seq, tm, c), whole)],
        out_specs=pl.BlockSpec((n_seq, tm, c), whole),
        out_shape=jax.ShapeDtypeStruct((n_seq, tm, c), BF16),
        scratch_shapes=[pltpu.VMEM((HALO + tm, c), F32)],
        compiler_params=_params("arbitrary"),
        name="pool_delta",
    )(history, p)


def _mix_finish(x_ref, oa_ref, deltas, pw_ref, ps_ref, wo_ref, g2_ref, x1_ref, h2_ref):
    da = oa_ref.shape[1]
    group = pw_ref.shape[1]
    y = jnp.dot(oa_ref[...], wo_ref[:da, :], preferred_element_type=F32)
    pooled = []
    for g, delta in enumerate(deltas):
        y_g = jnp.dot(delta, pw_ref[g], preferred_element_type=F32)
        pooled.append((y_g * ps_ref[:, g * group:(g + 1) * group]).astype(BF16))
    y = y + jnp.dot(jnp.concatenate(pooled, axis=1), wo_ref[da:, :], preferred_element_type=F32)
    x1 = x_ref[...] + y
    x1_ref[...] = x1
    ms = jnp.mean(x1 * x1, axis=-1, keepdims=True)
    h2_ref[...] = (x1 * lax.rsqrt(ms + EPS) * g2_ref[...]).astype(BF16)


def _mix_out_kernel(x_ref, oa_ref, d_ref, pw_ref, ps_ref, wo_ref, g2_ref, x1_ref, h2_ref):
    group = pw_ref.shape[1]
    deltas = [d_ref[:, g * group:(g + 1) * group] for g in range(pw_ref.shape[0])]
    _mix_finish(x_ref, oa_ref, deltas, pw_ref, ps_ref, wo_ref, g2_ref, x1_ref, h2_ref)


def _mix_out_pool_kernel(x_ref, oa_ref, halo_ref, p_ref, pw_ref, ps_ref, wo_ref, g2_ref, x1_ref, h2_ref,
                         ext_ref, *, blocks_per_seq):
    tm = p_ref.shape[0]
    seq_blk = pl.program_id(0) % blocks_per_seq
    ext_ref[pl.ds(0, HALO), :] = jnp.where(seq_blk == 0, 0.0, halo_ref[...])
    ext_ref[pl.ds(HALO, tm), :] = p_ref[...]
    pos = seq_blk * tm + lax.broadcasted_iota(jnp.int32, (tm, 1), 0)
    deltas = _window_deltas(ext_ref, tm, pos)
    _mix_finish(x_ref, oa_ref, deltas, pw_ref, ps_ref, wo_ref, g2_ref, x1_ref, h2_ref)


def _mix_out(x, oa, pool_in, pool_w, pool_scale, w_out, g2, tm, blocks_per_seq=None):
    rows, dm = x.shape
    da = oa.shape[1]
    dp = pool_in.shape[1]
    row = lambda i: (i, 0)
    fixed2 = lambda i: (0, 0)
    weight_specs = [
        pl.BlockSpec(pool_w.shape, lambda i: (0, 0, 0)),
        pl.BlockSpec((1, dp), fixed2),
        pl.BlockSpec(w_out.shape, fixed2),
        pl.BlockSpec((1, dm), fixed2),
    ]
    common = dict(
        grid=(rows // tm,),
        out_specs=[pl.BlockSpec((tm, dm), row), pl.BlockSpec((tm, dm), row)],
        out_shape=[jax.ShapeDtypeStruct((rows, dm), F32), jax.ShapeDtypeStruct((rows, dm), BF16)],
        compiler_params=_params("arbitrary"),
        name="mix_out",
    )
    if blocks_per_seq is None:
        return pl.pallas_call(
            _mix_out_kernel,
            in_specs=[pl.BlockSpec((tm, dm), row), pl.BlockSpec((tm, da), row), pl.BlockSpec((tm, dp), row)]
            + weight_specs,
            **common,
        )(x, oa, pool_in, pool_w, pool_scale, w_out, g2)
    halo_per_block = tm // HALO
    return pl.pallas_call(
        functools.partial(_mix_out_pool_kernel, blocks_per_seq=blocks_per_seq),
        in_specs=[
            pl.BlockSpec((tm, dm), row),
            pl.BlockSpec((tm, da), row),
            pl.BlockSpec((None, HALO, dp), lambda i: (jnp.maximum(i * halo_per_block - 1, 0), 0, 0)),
            pl.BlockSpec((tm, dp), row),
        ] + weight_specs,
        scratch_shapes=[pltpu.VMEM((HALO + tm, dp), F32)],
        **common,
    )(x, oa, pool_in.reshape(rows // HALO, HALO, dp), pool_in, pool_w, pool_scale, w_out, g2)


def _ffn_body(x1_ref, h2_ref, weight, o_ref):
    @pl.when(pl.program_id(1) == 0)
    def _():
        o_ref[...] = x1_ref[...]

    h2 = h2_ref[...]
    g = jnp.dot(h2, weight("gate"), preferred_element_type=F32)
    u = jnp.dot(h2, weight("up"), preferred_element_type=F32)
    ff = (g / (1.0 + jnp.exp(-g))) * u
    o_ref[...] += jnp.dot(ff.astype(BF16), weight("down"), preferred_element_type=F32)


def _ffn_kernel(x1_ref, h2_ref, wg_ref, wu_ref, wd_ref, o_ref):
    refs = {"gate": wg_ref, "up": wu_ref, "down": wd_ref}
    _ffn_body(x1_ref, h2_ref, lambda name: refs[name][...], o_ref)


def _ffn_cast_kernel(x1_ref, h2_ref, wg_ref, wu_ref, wd_ref, o_ref, wgb_ref, wub_ref, wdb_ref):
    refs = {"gate": (wg_ref, wgb_ref), "up": (wu_ref, wub_ref), "down": (wd_ref, wdb_ref)}

    def weight(name):
        src, dst = refs[name]
        w = src[...].astype(BF16)
        dst[...] = w
        return w

    _ffn_body(x1_ref, h2_ref, weight, o_ref)


def _ffn(x1, h2, wg, wu, wd, tm, tf):
    rows, dm = x1.shape
    dff = wg.shape[1]
    cast = wg.dtype == F32
    assert not cast or rows == tm, "every weight block must be visited exactly once to be cast"
    weight_specs = [
        pl.BlockSpec((dm, tf), lambda i, f: (0, f)),
        pl.BlockSpec((dm, tf), lambda i, f: (0, f)),
        pl.BlockSpec((tf, dm), lambda i, f: (f, 0)),
    ]
    out_specs = [pl.BlockSpec((tm, dm), lambda i, f: (i, 0))]
    out_shape = [jax.ShapeDtypeStruct((rows, dm), F32)]
    if cast:
        out_specs += weight_specs
        out_shape += [jax.ShapeDtypeStruct(w.shape, BF16) for w in (wg, wu, wd)]
    out = pl.pallas_call(
        _ffn_cast_kernel if cast else _ffn_kernel,
        grid=(rows // tm, dff // tf),
        in_specs=[pl.BlockSpec((tm, dm), lambda i, f: (i, 0)), pl.BlockSpec((tm, dm), lambda i, f: (i, 0))]
        + weight_specs,
        out_specs=out_specs,
        out_shape=out_shape,
        compiler_params=_params("arbitrary", "arbitrary"),
        name="ffn",
    )(x1, h2, wg, wu, wd)
    return out if cast else out[0]


def _sample_attn_kernel(pt_ref, q_ref, kn_ref, vn_ref, bias_ref, *refs, n_heads, page, pages_per_step):
    del pt_ref
    k_refs = refs[:pages_per_step]
    v_refs = refs[pages_per_step:2 * pages_per_step]
    o_ref, acc_ref, run_ref, qbd_ref, kpad_ref, vpad_ref = refs[2 * pages_per_step:]
    j = pl.program_id(1)
    n_q = q_ref.shape[0]
    rows = n_q * n_heads
    da = n_heads * HEAD_DIM
    group = 2 * page
    n_groups = pages_per_step // 2
    head_of_row = lax.broadcasted_iota(jnp.int32, (rows, da), 0) % n_heads
    head_of_col = lax.broadcasted_iota(jnp.int32, (rows, da), 1) // HEAD_DIM
    own_head = head_of_row == head_of_col
    bias = bias_ref[...] * LOG2E

    def scores(keys, n_tiles):
        y = lax.dot_general(qbd_ref[...], keys, (((1,), (1,)), ((), ())), preferred_element_type=F32)
        return y + jnp.concatenate([bias] * n_tiles, axis=1)

    def heads_side_by_side(ref):
        return jnp.concatenate(
            [ref[pl.ds(h, page, stride=n_heads), :] for h in range(n_heads)], axis=1).astype(BF16)

    @pl.when(j == 0)
    def _():
        q = q_ref[...]
        q_rows = jnp.concatenate(
            [jnp.broadcast_to(q[t:t + 1, :], (n_heads, da)) for t in range(n_q)], axis=0)
        qbd_ref[...] = jnp.where(own_head, q_rows, 0.0).astype(BF16)
        n_new = kn_ref.shape[0]
        kpad_ref[...] = jnp.zeros(kpad_ref.shape, F32)
        vpad_ref[...] = jnp.zeros(vpad_ref.shape, F32)
        kpad_ref[pl.ds(0, n_new), :] = kn_ref[...]
        vpad_ref[pl.ds(0, n_new), :] = vn_ref[...]
        t_of_row = lax.broadcasted_iota(jnp.int32, (rows, page), 0) // n_heads
        key = lax.broadcasted_iota(jnp.int32, (rows, page), 1)
        causal = key < t_of_row
        y = scores(kpad_ref[...].astype(BF16), 1)
        sp = jnp.where(causal, _softplus2(y), 0.0)
        rem = jnp.dot(_split_bf16(sp), _minus_later_keys(page), preferred_element_type=F32)
        a = jnp.where(causal, jnp.exp2(y - sp + rem), 0.0)
        acc_ref[...] = jnp.dot(a.astype(BF16), vpad_ref[...].astype(BF16), preferred_element_type=F32)
        run_ref[...] = -jnp.sum(sp, axis=-1, keepdims=True)

    keys = jnp.concatenate([heads_side_by_side(r) for r in k_refs], axis=0)
    y = scores(keys, pages_per_step)
    sp = _softplus2(y)
    split = jnp.concatenate(
        [_split_bf16(sp[:, g * group:(g + 1) * group]) for g in range(n_groups)], axis=0)
    rem = jnp.dot(split, _minus_later_keys(group, page), preferred_element_type=F32)
    run = run_ref[...]
    weights = []
    for g in range(n_groups):
        cols = slice(g * group, (g + 1) * group)
        a = jnp.exp2(y[:, cols] - sp[:, cols] + rem[g * rows:(g + 1) * rows]) * jnp.exp2(run)
        weights.append(a.astype(BF16))
        run = run - jnp.sum(sp[:, cols], axis=-1, keepdims=True)
    run_ref[...] = run
    values = jnp.concatenate([heads_side_by_side(r) for r in v_refs], axis=0)
    acc = acc_ref[...] + jnp.dot(jnp.concatenate(weights, axis=1), values, preferred_element_type=F32)
    acc_ref[...] = acc

    @pl.when(j == pl.num_programs(1) - 1)
    def _():
        kept = jnp.where(own_head, acc, 0.0)
        o_ref[...] = jnp.sum(kept.reshape(n_q, n_heads, da), axis=1)


def _sample_attn(q, k_new, v_new, bias_rows, cache_k, cache_v, page_table):
    b, n_q, da = q.shape
    n_heads = da // HEAD_DIM
    rows = n_q * n_heads
    n_pages = page_table.shape[1]
    page = cache_k.shape[1] // n_heads
    pps = PAGES_PER_STEP
    steps = n_pages // pps

    def page_spec(i):
        return pl.BlockSpec(
            (None, page * n_heads, HEAD_DIM),
            lambda bi, j, pt, i=i: (pt[bi, n_pages - 1 - (j * pps + i)], 0, 0))

    per_batch = lambda bi, j, pt: (bi, 0, 0)
    grid_spec = pltpu.PrefetchScalarGridSpec(
        num_scalar_prefetch=1,
        grid=(b, steps),
        in_specs=[
            pl.BlockSpec((None, n_q, da), per_batch),
            pl.BlockSpec((None, k_new.shape[1], da), per_batch),
            pl.BlockSpec((None, v_new.shape[1], da), per_batch),
            pl.BlockSpec((rows, page), lambda bi, j, pt: (0, 0)),
        ] + [page_spec(i) for i in range(pps)] * 2,
        out_specs=pl.BlockSpec((None, n_q, da), per_batch),
        scratch_shapes=[
            pltpu.VMEM((rows, da), F32),
            pltpu.VMEM((rows, 1), F32),
            pltpu.VMEM((rows, da), BF16),
            pltpu.VMEM((page, da), F32),
            pltpu.VMEM((page, da), F32),
        ],
    )
    return pl.pallas_call(
        functools.partial(_sample_attn_kernel, n_heads=n_heads, page=page, pages_per_step=pps),
        grid_spec=grid_spec,
        out_shape=jax.ShapeDtypeStruct((b, n_q, da), F32),
        compiler_params=_params("arbitrary", "arbitrary"),
        name="sample_attn",
    )(page_table, q, k_new, v_new, bias_rows, *([cache_k] * pps), *([cache_v] * pps))


def _row_tile(rows, target):
    return target if rows % target == 0 else rows


def kernel(x_prompt, x_sample, cache_k, cache_v, state_pool, page_table, norm1_g, w_in, q_norm_g, k_norm_g,
           sb_bias, pool_w, pool_scale, w_out, norm2_g, w_gate, w_up, w_down):
    b_p, seq, dm = x_prompt.shape
    b_s, dec_seq, _ = x_sample.shape
    depth = w_in.shape[0]
    n_pool_pages, page, n_heads, _ = cache_k.shape[1:]
    da = n_heads * HEAD_DIM
    dp = state_pool.shape[-1]
    past_len = page_table.shape[1] * page

    xp = x_prompt.reshape(b_p * seq, dm)
    xs = x_sample.reshape(b_s * dec_seq, dm)
    tm_p = _row_tile(b_p * seq, 512)
    tm_s = b_s * dec_seq
    outs = [[] for _ in range(6)]
    for l in range(depth):
        g1 = norm1_g[l].reshape(1, dm)
        g2 = norm2_g[l].reshape(1, dm)
        qg = q_norm_g[l].reshape(1, HEAD_DIM)
        kg = k_norm_g[l].reshape(1, HEAD_DIM)
        pool_w_l = pool_w[l].astype(BF16)
        pool_scale_l = pool_scale[l].reshape(1, dp)
        w_out_l = w_out[l].astype(BF16)

        q, k, v, p, w_in_l = _in_proj(xs, g1, w_in[l], qg, kg, tm_s)
        pad = 8 - dec_seq
        k_new = jnp.pad(k.reshape(b_s, dec_seq, da), ((0, 0), (0, pad), (0, 0)))
        v_new = jnp.pad(v.reshape(b_s, dec_seq, da), ((0, 0), (0, pad), (0, 0)))
        bias_rows = jnp.broadcast_to(jnp.tile(sb_bias[l].astype(F32), dec_seq)[:, None], (dec_seq * n_heads, page))
        o_attn = _sample_attn(q.astype(F32).reshape(b_s, dec_seq, da), k_new, v_new, bias_rows,
                              cache_k[l].reshape(n_pool_pages, page * n_heads, HEAD_DIM),
                              cache_v[l].reshape(n_pool_pages, page * n_heads, HEAD_DIM), page_table)
        p3 = p.reshape(b_s, dec_seq, dp)
        hist = jnp.pad(state_pool[l], ((0, 0), (HALO - POOL_HIST, 0), (0, 0)))
        d = _pool_delta(hist, jnp.pad(p3, ((0, 0), (0, pad), (0, 0))), past_len)
        outs[3].append(k.reshape(b_s, dec_seq, n_heads, HEAD_DIM))
        outs[4].append(v.reshape(b_s, dec_seq, n_heads, HEAD_DIM))
        outs[5].append(jnp.concatenate([state_pool[l], p3], axis=1)[:, dec_seq:])
        x1, h2 = _mix_out(xs, o_attn.astype(BF16).reshape(b_s * dec_seq, da),
                          d[:, :dec_seq].reshape(b_s * dec_seq, dp), pool_w_l, pool_scale_l, w_out_l, g2, tm_s)
        xs, w_gate_l, w_up_l, w_down_l = _ffn(x1, h2, w_gate[l], w_up[l], w_down[l], tm_s, 512)

        q, k, v, p = _in_proj(xp, g1, w_in_l, qg, kg, tm_p)
        o_attn = _prompt_attn(q.reshape(b_p, seq, da), k.reshape(b_p, seq, da), v.reshape(b_p, seq, da),
                              sb_bias[l], 256)
        outs[0].append(k.reshape(b_p, seq, n_heads, HEAD_DIM))
        outs[1].append(v.reshape(b_p, seq, n_heads, HEAD_DIM))
        outs[2].append(p.reshape(b_p, seq, dp)[:, seq - POOL_HIST:])
        x1, h2 = _mix_out(xp, o_attn.reshape(b_p * seq, da), p, pool_w_l, pool_scale_l, w_out_l, g2, tm_p,
                          seq // tm_p)
        xp = _ffn(x1, h2, w_gate_l, w_up_l, w_down_l, tm_p, 512)

    stack = lambda xs_: jnp.stack(xs_, axis=0)
    return (xp.reshape(b_p, seq, dm), xs.reshape(b_s, dec_seq, dm), stack(outs[0]), stack(outs[1]),
            stack(outs[2]), stack(outs[3]), stack(outs[4]), stack(outs[5]))
```
